```python
import math, functools
import jax, jax.numpy as jnp
from jax import lax
import numpy as np

D_MODEL = 1024
BATCH = 8
SEQ = 8192
DEPTH = 1

CTX_LEN = 256
GRID_W = 64
HG_HEADS = 4
HG_DK = 128
HG_DV = 128
HG_WIDTH = HG_HEADS * HG_DV
HG_COLS = 5 * HG_WIDTH
HG_CHUNK = 64
RW_HEADS = 8
RW_HEAD = 64
RW_WIDTH = RW_HEADS * RW_HEAD
RW_W_LORA = 32
RW_A_LORA = 32
RW_G_LORA = 96
RW_COLS = 3 * RW_WIDTH + 2 * RW_W_LORA + 2 * RW_A_LORA + RW_G_LORA
IN_COLS = HG_COLS + RW_COLS
MIX_WIDTH = HG_WIDTH + RW_WIDTH
D_FF = int(math.ceil(8 * D_MODEL / 3 / 256)) * 256
ADA_EPS = 1e-6
LN_EPS = 1e-5
HG_NORM_EPS = 1e-5
RW_GN_EPS = 64e-5

kernel_name = 'hybrid_hgrn2_rwkv7_flow_block'


def _ln(x, eps=ADA_EPS):
    xf = x.astype(jnp.float32)
    m = jnp.mean(xf, -1, keepdims=True)
    v = jnp.mean(jnp.square(xf - m), -1, keepdims=True)
    return (xf - m) * lax.rsqrt(v + eps)


def _post_ln(x, g, b):
    return (_ln(x, LN_EPS) * g + b).astype(x.dtype)


def _modulate(x, shift, scale):
    return (_ln(x) * (1.0 + scale) + shift).astype(x.dtype)


def _rev(a):
    return jnp.flip(a, axis=1)


def _shift_seq(z):
    h = z.shape[-1] // 2
    p = jnp.pad(z, ((0, 0), (1, 1), (0, 0)))
    return jnp.concatenate([p[:, :-2, :h], p[:, 2:, h:]], axis=-1)


def _qshift_grid(z, rows):
    B, T, C = z.shape
    q = C // 4
    g = jnp.pad(z.reshape(B, rows, GRID_W, C), ((0, 0), (1, 1), (1, 1), (0, 0)))
    left = g[:, 1:-1, :-2, :q]
    right = g[:, 1:-1, 2:, q:2 * q]
    up = g[:, :-2, 1:-1, 2 * q:3 * q]
    down = g[:, 2:, 1:-1, 3 * q:]
    return jnp.concatenate([left, right, up, down], axis=-1).reshape(B, T, C)


def _hgrn2_chunk_scan(q, k, log_f, v, s0):
    B, T, H, DK = q.shape
    DV = v.shape[-1]
    n = T // HG_CHUNK

    def chunks(a):
        return a.reshape(B, n, HG_CHUNK, H, a.shape[-1]).transpose(1, 0, 3, 2, 4)

    mask = jnp.tril(jnp.ones((HG_CHUNK, HG_CHUNK), bool))[:, :, None]

    def step(s, inp):
        qc, kc, gc, vc = inp
        b = jnp.cumsum(gc, axis=2)
        diff = jnp.where(mask, b[:, :, :, None, :] - b[:, :, None, :, :], -jnp.inf)
        scores = jnp.einsum('bhtk,bhtsk->bhts', qc, jnp.exp(diff) * kc[:, :, None, :, :])
        o = (jnp.einsum('bhts,bhsv->bhtv', scores, vc)
             + jnp.einsum('bhtk,bhkv->bhtv', qc * jnp.exp(b), s))
        b_end = b[:, :, -1:, :]
        s = (jnp.exp(b_end[:, :, 0, :])[..., None] * s
             + jnp.einsum('bhsk,bhsv->bhkv', kc * jnp.exp(b_end - b), vc))
        return s, o

    s, o = lax.scan(step, s0, (chunks(q), chunks(k), chunks(log_f), chunks(v)))
    return o.transpose(1, 0, 3, 2, 4).reshape(B, T, H, DV), s


def _rwkv7_scan(r, w, k, v, a, b, s0):
    def step(s, inp):
        rt, wt, kt, vt, at, bt = inp
        sa = jnp.einsum('bhvk,bhk->bhv', s, at)
        s = s * wt[:, :, None, :] + sa[..., None] * bt[:, :, None, :] + vt[..., None] * kt[:, :, None, :]
        return s, jnp.einsum('bhvk,bhk->bhv', s, rt)

    xs = tuple(jnp.swapaxes(z, 0, 1) for z in (r, w, k, v, a, b))
    s, y = lax.scan(step, s0, xs)
    return jnp.swapaxes(y, 0, 1), s


def _token_mix(u, shift_fn, w_in_l, mu_l, lb_l, hg_norm_w_l, w0_l, w2_l, a0_l, a2_l, g2_l,
               kk_l, ka_l, rk_l, lnw_l, lnb_l, states):
    B, T, _ = u.shape
    f32 = jnp.float32
    z = jnp.einsum('btd,dc->btc', u, w_in_l).astype(f32)
    zh = z[..., :HG_COLS]
    zr = z[..., HG_COLS:]
    zr = zr + mu_l * (shift_fn(zr) - zr)
    hs_f, hs_b, rs_f, rs_b = states

    q, f_fwd, f_bwd, i, og = jnp.split(zh, 5, axis=-1)
    hg = lambda a: a.reshape(B, T, HG_HEADS, -1)
    q = hg(jax.nn.silu(q))
    i = hg(i)
    ff = lb_l[0] + (1.0 - lb_l[0]) * jax.nn.sigmoid(f_fwd)
    fb = lb_l[1] + (1.0 - lb_l[1]) * jax.nn.sigmoid(f_bwd)
    o_f, hs_f = _hgrn2_chunk_scan(q, hg(1.0 - ff), hg(jnp.log(ff)), i, hs_f)
    o_b, hs_b = _hgrn2_chunk_scan(_rev(q), _rev(hg(1.0 - fb)), _rev(hg(jnp.log(fb))), _rev(i), hs_b)
    o = o_f + _rev(o_b)
    o = o * lax.rsqrt(jnp.mean(jnp.square(o), -1, keepdims=True) + HG_NORM_EPS) * hg_norm_w_l
    y_hg = (o * jax.nn.silu(hg(og))).reshape(B, T, HG_WIDTH)

    W = RW_WIDTH
    r, k, v = zr[..., :W], zr[..., W:2 * W], zr[..., 2 * W:3 * W]
    o1 = 3 * W
    o2 = o1 + 2 * RW_W_LORA
    o3 = o2 + 2 * RW_A_LORA
    wd = zr[..., o1:o2].reshape(B, T, 2, RW_W_LORA)
    ad = zr[..., o2:o3].reshape(B, T, 2, RW_A_LORA)
    gd = zr[..., o3:]
    w_pre = w0_l + jnp.einsum('btel,elc->btec', jnp.tanh(wd), w2_l)
    decay = jnp.exp(-jnp.exp(-jax.nn.softplus(-w_pre) - 0.5))
    a = jax.nn.sigmoid(a0_l + jnp.einsum('btel,elc->btec', ad, a2_l))
    g = jnp.einsum('btl,lc->btc', jax.nn.sigmoid(gd), g2_l)
    rh = lambda t_: t_.reshape(B, T, RW_HEADS, RW_HEAD)
    kk = rh(k * kk_l)
    kk = kk / jnp.maximum(jnp.sqrt(jnp.sum(jnp.square(kk), -1, keepdims=True)), 1e-12)
    kd = k[:, :, None, :] * (1.0 + (a - 1.0) * ka_l)
    r_h, v_h = rh(r), rh(v)
    k_f, k_b = rh(kd[:, :, 0]), rh(kd[:, :, 1])
    a_f, a_b = rh(a[:, :, 0]), rh(a[:, :, 1])
    y_f, rs_f = _rwkv7_scan(r_h, rh(decay[:, :, 0]), k_f, v_h, -kk, kk * a_f, rs_f)
    y_b, rs_b = _rwkv7_scan(_rev(r_h), _rev(rh(decay[:, :, 1])), _rev(k_b), _rev(v_h),
                            _rev(-kk), _rev(kk * a_b), rs_b)
    yr = y_f + _rev(y_b)
    m = jnp.mean(yr, -1, keepdims=True)
    var = jnp.mean(jnp.square(yr - m), -1, keepdims=True)
    yr = (yr - m) * lax.rsqrt(var + RW_GN_EPS) * lnw_l.reshape(RW_HEADS, RW_HEAD) + lnb_l.reshape(RW_HEADS, RW_HEAD)
    bonus = (jnp.sum(r_h * k_f * rk_l, -1, keepdims=True) + jnp.sum(r_h * k_b * rk_l, -1, keepdims=True)) * v_h
    y_rw = (yr + bonus).reshape(B, T, W) * g

    y = jnp.concatenate([y_hg, y_rw], axis=-1).astype(u.dtype)
    return y, (hs_f, hs_b, rs_f, rs_b)


def _swiglu(u, wg, wu, wd):
    h = jax.nn.silu(jnp.einsum('btd,df->btf', u, wg)) * jnp.einsum('btd,df->btf', u, wu)
    return jnp.einsum('btf,fd->btd', h, wd)


def setup_inputs(seed: int = 0) -> dict:
    key = jax.random.key(seed)
    ks = jax.random.split(key, 32)
    f32 = jnp.float32
    nrm = lambda k, s, sc: jax.random.normal(k, s, f32) * sc
    beta = (8.0 * DEPTH) ** -0.25
    L = DEPTH
    return {
        'x': nrm(ks[0], (BATCH, SEQ, D_MODEL), 1.0),
        'c': nrm(ks[1], (BATCH, D_MODEL), 1.0),
        'ctx': nrm(ks[2], (BATCH, CTX_LEN, D_MODEL), 1.0),
        'c_ctx': nrm(ks[3], (D_MODEL,), 1.0),
        'w_ada': nrm(ks[4], (L, D_MODEL, 6 * D_MODEL), D_MODEL ** -0.5),
        'b_ada': nrm(ks[5], (L, 6 * D_MODEL), 0.02),
        'w_in': nrm(ks[6], (L, D_MODEL, IN_COLS), D_MODEL ** -0.5),
        'hgrn_lb_logits': nrm(ks[7], (L + 1, 2, HG_WIDTH), 0.5),
        'hgrn_norm_w': 1.0 + nrm(ks[8], (L, HG_DV), 0.02),
        'rwkv_mu': jax.random.uniform(ks[9], (L, RW_COLS), f32, 0.0, 1.0),
        'rwkv_w0': jax.random.uniform(ks[10], (L, 2, RW_WIDTH), f32, -4.0, 0.0),
        'rwkv_w2': nrm(ks[11], (L, 2, RW_W_LORA, RW_WIDTH), 0.1),
        'rwkv_a0': nrm(ks[12], (L, 2, RW_WIDTH), 0.3),
        'rwkv_a2': nrm(ks[13], (L, 2, RW_A_LORA, RW_WIDTH), 0.3 * RW_A_LORA ** -0.5),
        'rwkv_g2': nrm(ks[14], (L, RW_G_LORA, RW_WIDTH), RW_G_LORA ** -0.5),
        'rwkv_k_k': 0.85 + nrm(ks[15], (L, RW_WIDTH), 0.05),
        'rwkv_k_a': 1.0 + nrm(ks[16], (L, RW_WIDTH), 0.05),
        'rwkv_r_k': nrm(ks[17], (L, RW_HEADS, RW_HEAD), 0.1),
        'rwkv_lnx_w': 1.0 + nrm(ks[18], (L, RW_WIDTH), 0.02),
        'rwkv_lnx_b': nrm(ks[19], (L, RW_WIDTH), 0.02),
        'w_out': nrm(ks[20], (L, MIX_WIDTH, D_MODEL), beta * MIX_WIDTH ** -0.5),
        'ln1_g': 1.0 + nrm(ks[21], (L, D_MODEL), 0.02),
        'ln1_b': nrm(ks[22], (L, D_MODEL), 0.02),
        'w_ffn_gate': nrm(ks[23], (L, D_MODEL, D_FF), D_MODEL ** -0.5),
        'w_ffn_up': nrm(ks[24], (L, D_MODEL, D_FF), D_MODEL ** -0.5),
        'w_ffn_down': nrm(ks[25], (L, D_FF, D_MODEL), beta * D_FF ** -0.5),
        'ln2_g': 1.0 + nrm(ks[26], (L, D_MODEL), 0.02),
        'ln2_b': nrm(ks[27], (L, D_MODEL), 0.02),
    }


def reference(x, c, ctx, c_ctx, w_ada, b_ada, w_in, hgrn_lb_logits, hgrn_norm_w, rwkv_mu, rwkv_w0,
              rwkv_w2, rwkv_a0, rwkv_a2, rwkv_g2, rwkv_k_k, rwkv_k_a, rwkv_r_k, rwkv_lnx_w, rwkv_lnx_b,
              w_out, ln1_g, ln1_b, w_ffn_gate, w_ffn_up, w_ffn_down, ln2_g, ln2_b):
    B = x.shape[0]
    rows = x.shape[1] // GRID_W
    latent_shift = functools.partial(_qshift_grid, rows=rows)
    alpha = (2.0 * DEPTH) ** 0.25
    lb_all = jnp.cumsum(jax.nn.softmax(hgrn_lb_logits.astype(jnp.float32), axis=0), axis=0)
    zero_states = (jnp.zeros((B, HG_HEADS, HG_DK, HG_DV), jnp.float32),
                   jnp.zeros((B, HG_HEADS, HG_DK, HG_DV), jnp.float32),
                   jnp.zeros((B, RW_HEADS, RW_HEAD, RW_HEAD), jnp.float32),
                   jnp.zeros((B, RW_HEADS, RW_HEAD, RW_HEAD), jnp.float32))
    for l in range(DEPTH):
        mod = jnp.einsum('bd,de->be', jax.nn.silu(c), w_ada[l]) + b_ada[l]
        sh1, sc1, g1, sh2, sc2, g2 = [m[:, None, :] for m in jnp.split(mod, 6, axis=-1)]
        mod_c = jnp.einsum('d,de->e', jax.nn.silu(c_ctx), w_ada[l]) + b_ada[l]
        ch1, cs1, cg1, ch2, cs2, cg2 = jnp.split(mod_c, 6)
        mix_w = (w_in[l], rwkv_mu[l], lb_all[l], hgrn_norm_w[l], rwkv_w0[l], rwkv_w2[l], rwkv_a0[l],
                 rwkv_a2[l], rwkv_g2[l], rwkv_k_k[l], rwkv_k_a[l], rwkv_r_k[l], rwkv_lnx_w[l], rwkv_lnx_b[l])
        y_ctx, ctx_states = _token_mix(_modulate(ctx, ch1, cs1), _shift_seq, *mix_w, zero_states)
        y, _ = _token_mix(_modulate(x, sh1, sc1), latent_shift, *mix_w, ctx_states)
        x = _post_ln(alpha * x + g1 * jnp.einsum('btm,md->btd', y, w_out[l]), ln1_g[l], ln1_b[l])
        ffn = _swiglu(_modulate(x, sh2, sc2), w_ffn_gate[l], w_ffn_up[l], w_ffn_down[l])
        x = _post_ln(alpha * x + g2 * ffn, ln2_g[l], ln2_b[l])
        if l < DEPTH - 1:
            ctx = _post_ln(alpha * ctx + cg1 * jnp.einsum('btm,md->btd', y_ctx, w_out[l]), ln1_g[l], ln1_b[l])
            ffn_c = _swiglu(_modulate(ctx, ch2, cs2), w_ffn_gate[l], w_ffn_up[l], w_ffn_down[l])
            ctx = _post_ln(alpha * ctx + cg2 * ffn_c, ln2_g[l], ln2_b[l])
    return x
```

```python
import functools
import math

import jax
import jax.numpy as jnp
from jax import lax
from jax.experimental import pallas as pl
from jax.experimental.pallas import tpu as pltpu

f32 = jnp.float32
bf16 = jnp.bfloat16

LANES = 128
GRID_W = 64
HG_HEADS = 4
RW_HEADS = 8
RW_HEAD = 64
HG_WIDTH = 512
RW_WIDTH = 512
HG_COLS = 5 * HG_WIDTH
RW_COLS = 1760
RW_COLS_PAD = 1792
LORA_TILE = 12
GLORA_TILE = 13
NPAIR = 4
CHUNK = 64
SUB = 16
ADA_EPS = 1e-6
LN_EPS = 1e-5
HG_NORM_EPS = 1e-5
RW_GN_EPS = 64e-5
EXP_M05 = math.exp(-0.5)
VMEM_LIMIT = 60 * 1024 * 1024

_NT = (((1,), (1,)), ((), ()))
_TN = (((0,), (0,)), ((), ()))


def _dot(a, b):
    return jnp.dot(a, b, preferred_element_type=f32)


def _dot_nt(a, b):
    return lax.dot_general(a, b, _NT, preferred_element_type=f32)


def _dot_tn(a, b):
    return lax.dot_general(a, b, _TN, preferred_element_type=f32)


def _ln(xf, eps):
    m = jnp.mean(xf, -1, keepdims=True)
    xc = xf - m
    v = jnp.mean(xc * xc, -1, keepdims=True)
    return xc * lax.rsqrt(v + eps)


def _silu(x):
    return x * jax.nn.sigmoid(x)


def _split3(x):
    x1 = x.astype(bf16)
    r1 = x - x1.astype(f32)
    x2 = r1.astype(bf16)
    x3 = (r1 - x2.astype(f32)).astype(bf16)
    return x1, x2, x3


def _seg64_sum(x, lane_lo):
    s_all = jnp.sum(x, -1, keepdims=True)
    s_lo = jnp.sum(jnp.where(lane_lo, x, 0.0), -1, keepdims=True)
    return jnp.where(lane_lo, s_lo, s_all - s_lo)


def _const_spec(shape):
    nd = len(shape)
    return pl.BlockSpec(shape, lambda *_: (0,) * nd, pipeline_mode=pl.Buffered(1))


def _mod_body(cs_ref, w_ref, b_ref, o_ref):
    a = _silu(cs_ref[...])
    a1, a2, _ = _split3(a)
    w1, w2, _ = _split3(w_ref[...])
    o_ref[...] = _dot(a1, w1) + _dot(a1, w2) + _dot(a2, w1) + b_ref[...]


def _mod_call(cs, w_ada, b_ada):
    rows, d = cs.shape
    n = w_ada.shape[1]
    tn = 1024
    return pl.pallas_call(
        _mod_body,
        grid=(n // tn,),
        in_specs=[
            pl.BlockSpec((rows, d), lambda j: (0, 0)),
            pl.BlockSpec((d, tn), lambda j: (0, j)),
            pl.BlockSpec((1, tn), lambda j: (0, j)),
        ],
        out_specs=pl.BlockSpec((rows, tn), lambda j: (0, j)),
        out_shape=jax.ShapeDtypeStruct((rows, n), f32),
        compiler_params=pltpu.CompilerParams(dimension_semantics=("parallel",)),
        name="mod",
    )(cs, w_ada, b_ada)


N_PREP = 16


def _inproj_body(*refs, tm, pad, grid_mode):
    if grid_mode:
        x_ref, xu_ref, xd_ref = refs[:3]
        rest = refs[3:]
    else:
        x_ref = refs[0]
        rest = refs[1:]
    (sh_ref, sc_ref, wh_ref, wr_ref, mu_ref, lb_ref, wl_ref, w0a0_ref, g2_ref, rwp_ref) = rest[:10]
    outs = rest[10:10 + N_PREP]
    ext_ref = rest[10 + N_PREP]
    (hq_o, hv_o, hff_o, hfb_o, hgate_o, r_o, v_o, kk_o, lwf_o, lwb_o, kf_o, kb_o, bf_o, bb_o, g_o, bonus_o) = outs

    sh = sh_ref[0]
    sc1 = 1.0 + sc_ref[0]

    def modulate(xv):
        return (_ln(xv, ADA_EPS) * sc1 + sh).astype(bf16)

    u_main = modulate(x_ref[0])
    wr = wr_ref[...]
    zr_main = _dot(u_main, wr)
    if grid_mode:
        i = pl.program_id(1)
        nt = pl.num_programs(1)
        zr_up = _dot(modulate(xu_ref[0]), wr)
        zr_dn = _dot(modulate(xd_ref[0]), wr)
        ext_ref[0:pad] = jnp.where(i > 0, zr_up, 0.0)
        ext_ref[pad + tm:pad + tm + pad] = jnp.where(i < nt - 1, zr_dn, 0.0)
    else:
        ext_ref[0:pad] = jnp.zeros((pad, RW_COLS_PAD), f32)
        ext_ref[pad + tm:pad + tm + pad] = jnp.zeros((pad, RW_COLS_PAD), f32)
    ext_ref[pad:pad + tm] = zr_main

    lane = lax.broadcasted_iota(jnp.int32, (tm, LANES), 1)
    if grid_mode:
        col = lax.broadcasted_iota(jnp.int32, (tm, LANES), 0) & (GRID_W - 1)
        not_first_col = col != 0
        not_last_col = col != GRID_W - 1
        bounds = [0, RW_COLS // 4, RW_COLS // 2, 3 * RW_COLS // 4, RW_COLS]
        windows = [(pad - 1, not_first_col), (pad + 1, not_last_col), (pad - GRID_W, None), (pad + GRID_W, None)]
    else:
        bounds = [0, RW_COLS // 2, RW_COLS]
        windows = [(pad - 1, None), (pad + 1, None)]

    zt = []
    for c in range(RW_COLS_PAD // LANES):
        lo, hi = c * LANES, (c + 1) * LANES
        cs = slice(lo, hi)
        z = zr_main[:, cs]
        shifted = jnp.zeros((tm, LANES), f32)
        for q, (off, valid) in enumerate(windows):
            qlo, qhi = bounds[q], bounds[q + 1]
            if max(lo, qlo) >= min(hi, qhi):
                continue
            win = ext_ref[off:off + tm, cs]
            m = valid
            if not (qlo <= lo and hi <= qhi):
                inq = (lane >= qlo - lo) & (lane < qhi - lo)
                m = inq if m is None else (m & inq)
            shifted = win if m is None else jnp.where(m, win, shifted)
        zt.append(z + mu_ref[:, cs] * (shifted - z))

    lane_lo = lane < RW_HEAD
    lora_in = jnp.where(lane_lo, jnp.tanh(zt[LORA_TILE]), zt[LORA_TILE]).astype(bf16)
    pre = _dot(lora_in, wl_ref[...]) + w0a0_ref[...]
    g_all = _dot(jax.nn.sigmoid(zt[GLORA_TILE]).astype(bf16), g2_ref[...])

    for p in range(NPAIR):
        sl = slice(p * LANES, (p + 1) * LANES)
        r = zt[p]
        k = zt[NPAIR + p]
        v = zt[2 * NPAIR + p]
        lw_f = -jax.nn.sigmoid(pre[:, sl]) * EXP_M05
        lw_b = -jax.nn.sigmoid(pre[:, RW_WIDTH + p * LANES:RW_WIDTH + (p + 1) * LANES]) * EXP_M05
        a_f = jax.nn.sigmoid(pre[:, 2 * RW_WIDTH + p * LANES:2 * RW_WIDTH + (p + 1) * LANES])
        a_b = jax.nn.sigmoid(pre[:, 3 * RW_WIDTH + p * LANES:3 * RW_WIDTH + (p + 1) * LANES])
        kk = k * rwp_ref[0:1, sl]
        kk = kk / jnp.maximum(jnp.sqrt(_seg64_sum(kk * kk, lane_lo)), 1e-12)
        ka = rwp_ref[1:2, sl]
        k_f = k * (1.0 + (a_f - 1.0) * ka)
        k_b = k * (1.0 + (a_b - 1.0) * ka)
        bonus = _seg64_sum(r * rwp_ref[2:3, sl] * (k_f + k_b), lane_lo) * v
        r_o[0, p] = r
        v_o[0, p] = v
        kk_o[0, p] = kk
        lwf_o[0, p] = lw_f
        lwb_o[0, p] = lw_b
        kf_o[0, p] = k_f
        kb_o[0, p] = k_b
        bf_o[0, p] = kk * a_f
        bb_o[0, p] = kk * a_b
        g_o[0, p] = g_all[:, sl]
        bonus_o[0, p] = bonus

    zh = _dot(u_main, wh_ref[...])
    for p in range(NPAIR):
        sl = slice(p * LANES, (p + 1) * LANES)
        lb_f = lb_ref[0:1, sl]
        lb_b = lb_ref[1:2, sl]
        hq_o[0, p] = _silu(zh[:, p * LANES:(p + 1) * LANES])
        hff_o[0, p] = lb_f + (1.0 - lb_f) * jax.nn.sigmoid(zh[:, HG_WIDTH + p * LANES:HG_WIDTH + (p + 1) * LANES])
        hfb_o[0, p] = lb_b + (1.0 - lb_b) * jax.nn.sigmoid(
            zh[:, 2 * HG_WIDTH + p * LANES:2 * HG_WIDTH + (p + 1) * LANES])
        hv_o[0, p] = zh[:, 3 * HG_WIDTH + p * LANES:3 * HG_WIDTH + (p + 1) * LANES]
        hgate_o[0, p] = _silu(zh[:, 4 * HG_WIDTH + p * LANES:4 * HG_WIDTH + (p + 1) * LANES])


def _inproj_call(x, shift, scale, wts, *, grid_mode):
    B, T, D = x.shape
    if grid_mode:
        tm, pad = 256, GRID_W
        rows_per_tile = tm // GRID_W
        n_rows = T // GRID_W
        x_specs = [
            pl.BlockSpec((1, tm, D), lambda b, i: (b, i, 0)),
            pl.BlockSpec((1, GRID_W, D), lambda b, i: (b, jnp.maximum(i * rows_per_tile - 1, 0), 0)),
            pl.BlockSpec((1, GRID_W, D), lambda b, i: (b, jnp.minimum((i + 1) * rows_per_tile, n_rows - 1), 0)),
        ]
        x_args = [x, x, x]
    else:
        tm, pad = T, 8
        x_specs = [pl.BlockSpec((1, tm, D), lambda b, i: (b, i, 0))]
        x_args = [x]
    vec_spec = pl.BlockSpec((1, 1, D), lambda b, i: (b, 0, 0))
    w_specs = [_const_spec(w.shape) for w in wts]
    out_spec = pl.BlockSpec((1, NPAIR, tm, LANES), lambda b, i: (b, 0, i, 0))
    out_shape = jax.ShapeDtypeStruct((B, NPAIR, T, LANES), f32)
    return pl.pallas_call(
        functools.partial(_inproj_body, tm=tm, pad=pad, grid_mode=grid_mode),
        grid=(B, T // tm),
        in_specs=x_specs + [vec_spec, vec_spec] + w_specs,
        out_specs=[out_spec] * N_PREP,
        out_shape=[out_shape] * N_PREP,
        scratch_shapes=[pltpu.VMEM((tm + 2 * pad, RW_COLS_PAD), f32)],
        compiler_params=pltpu.CompilerParams(
            dimension_semantics=("parallel", "parallel"), vmem_limit_bytes=VMEM_LIMIT),
        name="inproj_grid" if grid_mode else "inproj_seq",
    )(*x_args, shift, scale, *wts)


def _cumsum3(x, tri):
    x1, x2, x3 = _split3(x)
    return _dot(tri, x1) + _dot(tri, x2) + _dot(tri, x3)


def _hgrn_chunk(q, v, f, St, *, rev, emit, tri):
    C = CHUNK
    lf = jnp.log(f)
    k = 1.0 - f
    b = _cumsum3(lf, tri)
    e = 0 if rev else C - 1
    b_end = b[e:e + 1]
    kd = (k * jnp.exp(b_end - b)).astype(bf16)
    vb = v.astype(bf16)
    St_new = St * jnp.exp(b_end) + _dot_tn(vb, kd)
    if not emit:
        return None, St_new

    o_inter = _dot_nt((q * jnp.exp(b)).astype(bf16), St.astype(bf16))
    row8 = lax.broadcasted_iota(jnp.int32, (8, LANES), 0)
    o_blocks = []
    for i in range(C // SUB):
        r0 = SUB * i
        rows = slice(r0, r0 + SUB)
        acc = o_inter[rows]
        if not rev and i > 0:
            ref, cols = b[r0 - 1:r0], slice(0, r0)
        elif rev and i < C // SUB - 1:
            ref, cols = b[r0 + SUB:r0 + SUB + 1], slice(r0 + SUB, C)
        else:
            cols = None
        if cols is not None:
            qi = (q[rows] * jnp.exp(b[rows] - ref)).astype(bf16)
            kj = (k[cols] * jnp.exp(ref - b[cols])).astype(bf16)
            sc = _dot_nt(qi, kj)
            acc = acc + _dot(sc.astype(bf16), vb[cols])
        halves = [acc[0:8], acc[8:16]]
        for sl_ in range(SUB):
            s = r0 + sl_
            hs, sr = divmod(sl_, 8)
            bs, ks, vs = b[s:s + 1], k[s:s + 1], v[s:s + 1]
            for h in (0, 1):
                if (h < hs and not rev) or (h > hs and rev):
                    continue
                h0 = r0 + 8 * h
                d = b[h0:h0 + 8] - bs
                if h == hs:
                    d = jnp.where((row8 <= sr) if rev else (row8 >= sr), d, -jnp.inf)
                w = jnp.sum(q[h0:h0 + 8] * (ks * jnp.exp(d)), -1, keepdims=True)
                halves[h] = halves[h] + w * vs
        o_blocks += halves
    return jnp.concatenate(o_blocks, 0), St_new


def _tri_inv(L, eye, blk16, blk32):
    Lb = jnp.where(blk16, L, 0.0)
    T = eye + Lb
    Xb = Lb.astype(bf16)
    for _ in range(3):
        Xb = _dot(Xb, Xb).astype(bf16)
        T = T + _dot(T.astype(bf16), Xb)
    for off_mask in (blk32 & jnp.logical_not(blk16), jnp.logical_not(blk32)):
        off = jnp.where(off_mask, L, 0.0).astype(bf16)
        Tb = T.astype(bf16)
        T = T + _dot(_dot(Tb, off).astype(bf16), Tb)
    return T


def _rwkv_chunk(r, v, kk, lw, k, bb, Gt, *, rev, emit, tri, masks):
    C = CHUNK
    lane_lo, lo64, pmask, bdmask, eye, blk16, blk32 = masks
    g = _cumsum3(lw, tri)
    e = 0 if rev else C - 1
    g_end = jnp.exp(g[e:e + 1])
    ieg = jnp.exp(-g)
    at = -kk * jnp.exp(g - lw)
    BK = jnp.concatenate([bb * ieg, k * ieg], 0)
    if emit:
        AR = jnp.concatenate([at, r * jnp.exp(g)], 0).astype(bf16)
    else:
        AR = at.astype(bf16)
    P2 = _dot_nt(AR, Gt.astype(bf16))
    vb = v.astype(bf16)
    ZV = jnp.concatenate([jnp.zeros((C, LANES), bf16), vb], 0)
    laks, Ts, Mrs = [], [], []
    for h in (0, 1):
        BKh = jnp.where(lane_lo if h == 0 else jnp.logical_not(lane_lo), BK, 0.0).astype(bf16)
        P1 = jnp.where(pmask, _dot_nt(AR, BKh), 0.0)
        top = P1[0:C]
        laks.append(_dot(top.astype(bf16), ZV))
        L = top[:, 0:C]
        T = _tri_inv(L, eye, blk16, blk32)
        Ts.append(T.astype(bf16))
        if emit:
            Mrs.append(P1[C:2 * C].astype(bf16))
    Xp = (P2[0:C] + jnp.where(lo64, laks[0], laks[1])).astype(bf16)
    U = jnp.where(lo64, _dot(Ts[0], Xp), _dot(Ts[1], Xp))
    UV = jnp.concatenate([U.astype(bf16), vb], 0)
    Gt_new = Gt * g_end + jnp.where(bdmask, _dot_tn(UV, (BK * g_end).astype(bf16)), 0.0)
    if not emit:
        return None, Gt_new
    Y = P2[C:2 * C] + jnp.where(lo64, _dot(Mrs[0], UV), _dot(Mrs[1], UV))
    return Y, Gt_new


def _scan_body(*refs, emit):
    fwd = refs[0:9]
    bwd = refs[9:18]
    hs0_ref, gs0_ref = refs[18:20]
    if emit:
        ho_f, ry_f, ho_b, ry_b, hsT_ref, gsT_ref, hs, gs = refs[20:28]
    else:
        hsT_ref, gsT_ref, hs, gs = refs[20:24]
    j = pl.program_id(1)
    C = CHUNK

    @pl.when(j == 0)
    def _():
        hs[...] = hs0_ref[:, 0]
        gs[...] = gs0_ref[:, 0]

    ri = lax.broadcasted_iota(jnp.int32, (C, C), 0)
    ci = lax.broadcasted_iota(jnp.int32, (C, C), 1)
    tri_f = (ci <= ri).astype(bf16)
    tri_b = (ci >= ri).astype(bf16)
    rows = 2 * C if emit else C
    R = lax.broadcasted_iota(jnp.int32, (rows, 2 * C), 0)
    Cc = lax.broadcasted_iota(jnp.int32, (rows, 2 * C), 1)
    t, s = R & (C - 1), Cc & (C - 1)
    diag_ok = (R >= C) & (s == t)
    pmask_f = (s < t) | diag_ok
    pmask_b = (s > t) | diag_ok
    R2 = lax.broadcasted_iota(jnp.int32, (2 * C, 2 * C), 0)
    C2 = lax.broadcasted_iota(jnp.int32, (2 * C, 2 * C), 1)
    lane_lo = C2 < RW_HEAD
    bdmask = (R2 < RW_HEAD) == lane_lo
    lo64 = lax.broadcasted_iota(jnp.int32, (C, 2 * C), 1) < RW_HEAD
    eye = (ri == ci).astype(f32)
    blk16 = (ri // SUB) == (ci // SUB)
    blk32 = (ri // (2 * SUB)) == (ci // (2 * SUB))
    masks_f = (lane_lo, lo64, pmask_f, bdmask, eye, blk16, blk32)
    masks_b = (lane_lo, lo64, pmask_b, bdmask, eye, blk16, blk32)

    def body(p, carry):
        for d, (src, rev, tri, masks) in enumerate(((fwd, False, tri_f, masks_f), (bwd, True, tri_b, masks_b))):
            hq, hv, hf, r, v, kk, lw, k, bb = [ref[0, p] for ref in src]
            o, St_new = _hgrn_chunk(hq, hv, hf, hs[d, p], rev=rev, emit=emit, tri=tri)
            y, Gt_new = _rwkv_chunk(r, v, kk, lw, k, bb, gs[d, p], rev=rev, emit=emit, tri=tri, masks=masks)
            hs[d, p] = St_new
            gs[d, p] = Gt_new
            if emit:
                (ho_b if rev else ho_f)[0, p] = o
                (ry_b if rev else ry_f)[0, p] = y
        return carry

    lax.fori_loop(0, NPAIR, body, 0)

    @pl.when(j == pl.num_programs(1) - 1)
    def _():
        hsT_ref[:, 0] = hs[...]
        gsT_ref[:, 0] = gs[...]


def _scan_call(prep, hs0, gs0, *, emit):
    B, _, T, _ = prep["hq"].shape
    n = T // CHUNK
    f_spec = pl.BlockSpec((1, NPAIR, CHUNK, LANES), lambda b, j: (b, 0, j, 0))
    b_spec = pl.BlockSpec((1, NPAIR, CHUNK, LANES), lambda b, j: (b, 0, n - 1 - j, 0))
    st_spec = pl.BlockSpec((2, 1, NPAIR, LANES, LANES), lambda b, j: (0, b, 0, 0, 0))
    fwd = [prep[nm] for nm in ("hq", "hv", "hff", "r", "v", "kk", "lwf", "kf", "bf")]
    bwd = [prep[nm] for nm in ("hq", "hv", "hfb", "r", "v", "kk", "lwb", "kb", "bb")]
    seq_shape = jax.ShapeDtypeStruct((B, NPAIR, T, LANES), f32)
    st_shape = jax.ShapeDtypeStruct((2, B, NPAIR, LANES, LANES), f32)
    out_specs = ([f_spec, f_spec, b_spec, b_spec] if emit else []) + [st_spec, st_spec]
    out_shape = ([seq_shape] * 4 if emit else []) + [st_shape, st_shape]
    return pl.pallas_call(
        functools.partial(_scan_body, emit=emit),
        grid=(B, n),
        in_specs=[f_spec] * 9 + [b_spec] * 9 + [st_spec, st_spec],
        out_specs=out_specs,
        out_shape=out_shape,
        scratch_shapes=[pltpu.VMEM((2, NPAIR, LANES, LANES), f32), pltpu.VMEM((2, NPAIR, LANES, LANES), f32)],
        compiler_params=pltpu.CompilerParams(
            dimension_semantics=("parallel", "arbitrary"), vmem_limit_bytes=VMEM_LIMIT),
        name="scan_latent" if emit else "scan_ctx",
    )(*fwd, *bwd, hs0, gs0)


def _mixout_body(hof, hob, hgate, ryf, ryb, g, bonus, x_ref, g1_ref, wout_ref, hgw_ref, lnx_ref, ln1_ref, o_ref,
                 *, alpha):
    tm = x_ref.shape[1]
    lane_lo = lax.broadcasted_iota(jnp.int32, (tm, LANES), 1) < RW_HEAD
    y_hg, y_rw = [], []
    for p in range(NPAIR):
        o = hof[0, p] + hob[0, p]
        o = o * lax.rsqrt(jnp.mean(o * o, -1, keepdims=True) + HG_NORM_EPS) * hgw_ref[...]
        y_hg.append((o * hgate[0, p]).astype(bf16))
        yr = ryf[0, p] + ryb[0, p]
        m = _seg64_sum(yr, lane_lo) * (1.0 / RW_HEAD)
        yc = yr - m
        var = _seg64_sum(yc * yc, lane_lo) * (1.0 / RW_HEAD)
        sl = slice(p * LANES, (p + 1) * LANES)
        yn = yc * lax.rsqrt(var + RW_GN_EPS) * lnx_ref[0:1, sl] + lnx_ref[1:2, sl]
        y_rw.append(((yn + bonus[0, p]) * g[0, p]).astype(bf16))
    y = jnp.concatenate(y_hg + y_rw, -1)
    mix = _dot(y, wout_ref[...])
    xn = _ln(alpha * x_ref[0] + g1_ref[0] * mix, LN_EPS)
    o_ref[0] = xn * ln1_ref[0:1] + ln1_ref[1:2]


def _mixout_call(scan_outs, prep, x, g1, wout, hgw, lnx, ln1, *, alpha):
    B, T, D = x.shape
    tm = 512
    seq_spec = pl.BlockSpec((1, NPAIR, tm, LANES), lambda b, i: (b, 0, i, 0))
    x_spec = pl.BlockSpec((1, tm, D), lambda b, i: (b, i, 0))
    vec_spec = pl.BlockSpec((1, 1, D), lambda b, i: (b, 0, 0))
    ho_f, ry_f, ho_b, ry_b = scan_outs
    return pl.pallas_call(
        functools.partial(_mixout_body, alpha=alpha),
        grid=(B, T // tm),
        in_specs=[seq_spec] * 7 + [x_spec, vec_spec] + [_const_spec(a.shape) for a in (wout, hgw, lnx, ln1)],
        out_specs=x_spec,
        out_shape=jax.ShapeDtypeStruct((B, T, D), f32),
        compiler_params=pltpu.CompilerParams(
            dimension_semantics=("parallel", "parallel"), vmem_limit_bytes=VMEM_LIMIT),
        name="mix_out",
    )(ho_f, ho_b, prep["hgate"], ry_f, ry_b, prep["g"], prep["bonus"], x, g1, wout, hgw, lnx, ln1)


def _ffn_body(x_ref, sh_ref, sc_ref, g2_ref, wg_ref, wu_ref, wd_ref, ln2_ref, o_ref, *, alpha):
    x1 = x_ref[0]
    u = (_ln(x1, ADA_EPS) * (1.0 + sc_ref[0]) + sh_ref[0]).astype(bf16)
    h = (_silu(_dot(u, wg_ref[...])) * _dot(u, wu_ref[...])).astype(bf16)
    ffn = _dot(h, wd_ref[...])
    xn = _ln(alpha * x1 + g2_ref[0] * ffn, LN_EPS)
    o_ref[0] = xn * ln2_ref[0:1] + ln2_ref[1:2]


def _ffn_call(x1, sh2, sc2, g2, wg, wu, wd, ln2, *, alpha):
    B, T, D = x1.shape
    tm = 512
    x_spec = pl.BlockSpec((1, tm, D), lambda b, i: (b, i, 0))
    vec_spec = pl.BlockSpec((1, 1, D), lambda b, i: (b, 0, 0))
    return pl.pallas_call(
        functools.partial(_ffn_body, alpha=alpha),
        grid=(B, T // tm),
        in_specs=[x_spec, vec_spec, vec_spec, vec_spec] + [_const_spec(a.shape) for a in (wg, wu, wd, ln2)],
        out_specs=x_spec,
        out_shape=jax.ShapeDtypeStruct((B, T, D), f32),
        compiler_params=pltpu.CompilerParams(
            dimension_semantics=("parallel", "parallel"), vmem_limit_bytes=VMEM_LIMIT),
        name="ffn",
    )(x1, sh2, sc2, g2, wg, wu, wd, ln2)


_PREP_NAMES = ("hq", "hv", "hff", "hfb", "hgate", "r", "v", "kk", "lwf", "lwb", "kf", "kb", "bf", "bb", "g", "bonus")


def kernel(x, c, ctx, c_ctx, w_ada, b_ada, w_in, hgrn_lb_logits, hgrn_norm_w, rwkv_mu, rwkv_w0, rwkv_w2, rwkv_a0, rwkv_a2, rwkv_g2, rwkv_k_k, rwkv_k_a, rwkv_r_k, rwkv_lnx_w, rwkv_lnx_b, w_out, ln1_g, ln1_b, w_ffn_gate, w_ffn_up, w_ffn_down, ln2_g, ln2_b):
    B, T, D = x.shape
    depth = w_ada.shape[0]
    assert depth == 1 and T % 256 == 0 and ctx.shape[1] % CHUNK == 0
    alpha = (2.0 * depth) ** 0.25

    lb = jnp.cumsum(jax.nn.softmax(hgrn_lb_logits.astype(f32), axis=0), axis=0)[0]
    w_h = w_in[0, :, :HG_COLS].astype(bf16)
    w_r = jnp.pad(w_in[0, :, HG_COLS:], ((0, 0), (0, RW_COLS_PAD - RW_COLS))).astype(bf16)
    mu = jnp.pad(rwkv_mu[0], (0, RW_COLS_PAD - RW_COLS))[None]
    w_lora = jnp.zeros((LANES, 4 * RW_WIDTH), f32)
    for blk, w in enumerate((rwkv_w2[0, 0], rwkv_w2[0, 1], rwkv_a2[0, 0], rwkv_a2[0, 1])):
        w_lora = w_lora.at[32 * blk:32 * (blk + 1), RW_WIDTH * blk:RW_WIDTH * (blk + 1)].set(w)
    w_lora = w_lora.astype(bf16)
    w0a0 = jnp.concatenate([rwkv_w0[0, 0], rwkv_w0[0, 1], rwkv_a0[0, 0], rwkv_a0[0, 1]])[None]
    g2p = jnp.pad(rwkv_g2[0], ((0, LANES - rwkv_g2.shape[1]), (0, 0))).astype(bf16)
    rwp = jnp.zeros((8, RW_WIDTH), f32).at[0].set(rwkv_k_k[0]).at[1].set(rwkv_k_a[0]).at[2].set(
        rwkv_r_k[0].reshape(-1))
    wts = (w_h, w_r, mu, lb, w_lora, w0a0, g2p, rwp)

    cs = jnp.zeros((16, D), f32).at[:B].set(c).at[B].set(c_ctx)
    mod = _mod_call(cs, w_ada[0], b_ada[0][None])
    sh1, sc1, g1, sh2, sc2, g2 = [m[:B, None, :] for m in jnp.split(mod, 6, axis=-1)]
    ch1, cs1 = [jnp.broadcast_to(m[B:B + 1, None, :], (B, 1, D)) for m in jnp.split(mod, 6, axis=-1)[:2]]

    prep_ctx = dict(zip(_PREP_NAMES, _inproj_call(ctx, ch1, cs1, wts, grid_mode=False)))
    zeros_state = jnp.zeros((2, B, NPAIR, LANES, LANES), f32)
    hs_ctx, gs_ctx = _scan_call(prep_ctx, zeros_state, zeros_state, emit=False)

    prep = dict(zip(_PREP_NAMES, _inproj_call(x, sh1, sc1, wts, grid_mode=True)))
    ho_f, ry_f, ho_b, ry_b, _, _ = _scan_call(prep, hs_ctx, gs_ctx, emit=True)

    lnx = jnp.stack([rwkv_lnx_w[0], rwkv_lnx_b[0]])
    ln1 = jnp.stack([ln1_g[0], ln1_b[0]])
    ln2 = jnp.stack([ln2_g[0], ln2_b[0]])
    x1 = _mixout_call((ho_f, ry_f, ho_b, ry_b), prep, x, g1, w_out[0].astype(bf16), hgrn_norm_w[0][None], lnx, ln1,
                      alpha=alpha)
    return _ffn_call(x1, sh2, sc2, g2, w_ffn_gate[0].astype(bf16), w_ffn_up[0].astype(bf16),
                     w_ffn_down[0].astype(bf16), ln2, alpha=alpha)
```

```python
import functools
import math

import jax
import jax.numpy as jnp
from jax import lax
from jax.experimental import pallas as pl
from jax.experimental.pallas import tpu as pltpu

f32 = jnp.float32
bf16 = jnp.bfloat16

LANES = 128
GRID_W = 64
HG_HEADS = 4
RW_HEADS = 8
RW_HEAD = 64
HG_WIDTH = 512
RW_WIDTH = 512
HG_COLS = 5 * HG_WIDTH
RW_COLS = 1760
RW_COLS_PAD = 1792
LORA_TILE = 12
GLORA_TILE = 13
NPAIR = 4
CHUNK = 64
SUB = 16
SCAN_BATCH = 2
ADA_EPS = 1e-6
LN_EPS = 1e-5
HG_NORM_EPS = 1e-5
RW_GN_EPS = 64e-5
EXP_M05 = math.exp(-0.5)
LOG2E = math.log2(math.e)
VMEM_LIMIT = 60 * 1024 * 1024

_NT = (((1,), (1,)), ((), ()))
_TN = (((0,), (0,)), ((), ()))


def _dot(a, b):
    return jnp.dot(a, b, preferred_element_type=f32)


def _dot_nt(a, b):
    return lax.dot_general(a, b, _NT, preferred_element_type=f32)


def _dot_tn(a, b):
    return lax.dot_general(a, b, _TN, preferred_element_type=f32)


def _ln(xf, eps):
    m = jnp.mean(xf, -1, keepdims=True)
    xc = xf - m
    v = jnp.mean(xc * xc, -1, keepdims=True)
    return xc * lax.rsqrt(v + eps)


def _silu(x):
    return x * jax.nn.sigmoid(x)


def _split3(x):
    x1 = x.astype(bf16)
    r1 = x - x1.astype(f32)
    x2 = r1.astype(bf16)
    x3 = (r1 - x2.astype(f32)).astype(bf16)
    return x1, x2, x3


def _seg64_sum(x, lane_lo):
    s_all = jnp.sum(x, -1, keepdims=True)
    s_lo = jnp.sum(jnp.where(lane_lo, x, 0.0), -1, keepdims=True)
    return jnp.where(lane_lo, s_lo, s_all - s_lo)


def _const_spec(shape):
    nd = len(shape)
    return pl.BlockSpec(shape, lambda *_: (0,) * nd, pipeline_mode=pl.Buffered(1))


def _mod_body(cs_ref, w_ref, b_ref, o_ref):
    a = _silu(cs_ref[...])
    a1, a2, _ = _split3(a)
    w1, w2, _ = _split3(w_ref[...])
    o_ref[...] = _dot(a1, w1) + _dot(a1, w2) + _dot(a2, w1) + b_ref[...]


def _mod_call(cs, w_ada, b_ada):
    rows, d = cs.shape
    n = w_ada.shape[1]
    tn = 1024
    return pl.pallas_call(
        _mod_body,
        grid=(n // tn,),
        in_specs=[
            pl.BlockSpec((rows, d), lambda j: (0, 0)),
            pl.BlockSpec((d, tn), lambda j: (0, j)),
            pl.BlockSpec((1, tn), lambda j: (0, j)),
        ],
        out_specs=pl.BlockSpec((rows, tn), lambda j: (0, j)),
        out_shape=jax.ShapeDtypeStruct((rows, n), f32),
        compiler_params=pltpu.CompilerParams(dimension_semantics=("parallel",)),
        name="mod",
    )(cs, w_ada, b_ada)


N_PREP = 16


def _inproj_body(*refs, tm, pad, grid_mode):
    if grid_mode:
        x_ref, xu_ref, xd_ref = refs[:3]
        rest = refs[3:]
    else:
        x_ref = refs[0]
        rest = refs[1:]
    (sh_ref, sc_ref, wh_ref, wr_ref, mu_ref, lb_ref, wl_ref, w0a0_ref, g2_ref, rwp_ref) = rest[:10]
    outs = rest[10:10 + N_PREP]
    ext_ref = rest[10 + N_PREP]
    (hq_o, hv_o, hff_o, hfb_o, hgate_o, r_o, v_o, kk_o, lwf_o, lwb_o, kf_o, kb_o, bf_o, bb_o, g_o, bonus_o) = outs

    sh = sh_ref[0]
    sc1 = 1.0 + sc_ref[0]

    def modulate(xv):
        return (_ln(xv, ADA_EPS) * sc1 + sh).astype(bf16)

    u_main = modulate(x_ref[0])
    wr = wr_ref[...]
    if grid_mode:
        i = pl.program_id(1)
        nt = pl.num_programs(1)
        zr_ext = _dot(jnp.concatenate([modulate(xu_ref[0]), u_main, modulate(xd_ref[0])], 0), wr)
        zr_main = zr_ext[pad:pad + tm]
        ext_ref[0:pad] = jnp.where(i > 0, zr_ext[0:pad], 0.0)
        ext_ref[pad + tm:pad + tm + pad] = jnp.where(i < nt - 1, zr_ext[pad + tm:pad + tm + pad], 0.0)
    else:
        zr_main = _dot(u_main, wr)
        ext_ref[0:pad] = jnp.zeros((pad, RW_COLS_PAD), f32)
        ext_ref[pad + tm:pad + tm + pad] = jnp.zeros((pad, RW_COLS_PAD), f32)
    ext_ref[pad:pad + tm] = zr_main

    lane = lax.broadcasted_iota(jnp.int32, (tm, LANES), 1)
    if grid_mode:
        col = lax.broadcasted_iota(jnp.int32, (tm, LANES), 0) & (GRID_W - 1)
        not_first_col = col != 0
        not_last_col = col != GRID_W - 1
        bounds = [0, RW_COLS // 4, RW_COLS // 2, 3 * RW_COLS // 4, RW_COLS]
        windows = [(pad - 1, not_first_col), (pad + 1, not_last_col), (pad - GRID_W, None), (pad + GRID_W, None)]
    else:
        bounds = [0, RW_COLS // 2, RW_COLS]
        windows = [(pad - 1, None), (pad + 1, None)]

    zt = []
    for c in range(RW_COLS_PAD // LANES):
        lo, hi = c * LANES, (c + 1) * LANES
        cs = slice(lo, hi)
        z = zr_main[:, cs]
        shifted = jnp.zeros((tm, LANES), f32)
        for q, (off, valid) in enumerate(windows):
            qlo, qhi = bounds[q], bounds[q + 1]
            if max(lo, qlo) >= min(hi, qhi):
                continue
            win = ext_ref[off:off + tm, cs]
            m = valid
            if not (qlo <= lo and hi <= qhi):
                inq = (lane >= qlo - lo) & (lane < qhi - lo)
                m = inq if m is None else (m & inq)
            shifted = win if m is None else jnp.where(m, win, shifted)
        zt.append(z + mu_ref[:, cs] * (shifted - z))

    lane_lo = lane < RW_HEAD
    lora_in = jnp.where(lane_lo, jnp.tanh(zt[LORA_TILE]), zt[LORA_TILE]).astype(bf16)
    pre = _dot(lora_in, wl_ref[...]) + w0a0_ref[...]
    g_all = _dot(jax.nn.sigmoid(zt[GLORA_TILE]).astype(bf16), g2_ref[...])

    for p in range(NPAIR):
        sl = slice(p * LANES, (p + 1) * LANES)
        r = zt[p]
        k = zt[NPAIR + p]
        v = zt[2 * NPAIR + p]
        lw_f = -jax.nn.sigmoid(pre[:, sl]) * EXP_M05
        lw_b = -jax.nn.sigmoid(pre[:, RW_WIDTH + p * LANES:RW_WIDTH + (p + 1) * LANES]) * EXP_M05
        a_f = jax.nn.sigmoid(pre[:, 2 * RW_WIDTH + p * LANES:2 * RW_WIDTH + (p + 1) * LANES])
        a_b = jax.nn.sigmoid(pre[:, 3 * RW_WIDTH + p * LANES:3 * RW_WIDTH + (p + 1) * LANES])
        kk = k * rwp_ref[0:1, sl]
        kk = kk / jnp.maximum(jnp.sqrt(_seg64_sum(kk * kk, lane_lo)), 1e-12)
        ka = rwp_ref[1:2, sl]
        k_f = k * (1.0 + (a_f - 1.0) * ka)
        k_b = k * (1.0 + (a_b - 1.0) * ka)
        bonus = _seg64_sum(r * rwp_ref[2:3, sl] * (k_f + k_b), lane_lo) * v
        r_o[0, p] = r
        v_o[0, p] = v
        kk_o[0, p] = kk
        lwf_o[0, p] = lw_f
        lwb_o[0, p] = lw_b
        kf_o[0, p] = k_f
        kb_o[0, p] = k_b
        bf_o[0, p] = kk * a_f
        bb_o[0, p] = kk * a_b
        g_o[0, p] = g_all[:, sl]
        bonus_o[0, p] = bonus

    zh = _dot(u_main, wh_ref[...])
    for p in range(NPAIR):
        sl = slice(p * LANES, (p + 1) * LANES)
        lb_f = lb_ref[0:1, sl]
        lb_b = lb_ref[1:2, sl]
        hq_o[0, p] = _silu(zh[:, p * LANES:(p + 1) * LANES])
        hff_o[0, p] = lb_f + (1.0 - lb_f) * jax.nn.sigmoid(zh[:, HG_WIDTH + p * LANES:HG_WIDTH + (p + 1) * LANES])
        hfb_o[0, p] = lb_b + (1.0 - lb_b) * jax.nn.sigmoid(
            zh[:, 2 * HG_WIDTH + p * LANES:2 * HG_WIDTH + (p + 1) * LANES])
        hv_o[0, p] = zh[:, 3 * HG_WIDTH + p * LANES:3 * HG_WIDTH + (p + 1) * LANES]
        hgate_o[0, p] = _silu(zh[:, 4 * HG_WIDTH + p * LANES:4 * HG_WIDTH + (p + 1) * LANES])


def _inproj_call(x, shift, scale, wts, *, grid_mode):
    B, T, D = x.shape
    if grid_mode:
        tm, pad = 256, GRID_W
        rows_per_tile = tm // GRID_W
        n_rows = T // GRID_W
        x_specs = [
            pl.BlockSpec((1, tm, D), lambda b, i: (b, i, 0)),
            pl.BlockSpec((1, GRID_W, D), lambda b, i: (b, jnp.maximum(i * rows_per_tile - 1, 0), 0)),
            pl.BlockSpec((1, GRID_W, D), lambda b, i: (b, jnp.minimum((i + 1) * rows_per_tile, n_rows - 1), 0)),
        ]
        x_args = [x, x, x]
    else:
        tm, pad = T, 8
        x_specs = [pl.BlockSpec((1, tm, D), lambda b, i: (b, i, 0))]
        x_args = [x]
    vec_spec = pl.BlockSpec((1, 1, D), lambda b, i: (b, 0, 0))
    w_specs = [_const_spec(w.shape) for w in wts]
    out_spec = pl.BlockSpec((1, NPAIR, tm, LANES), lambda b, i: (b, 0, i, 0))
    out_shape = jax.ShapeDtypeStruct((B, NPAIR, T, LANES), f32)
    return pl.pallas_call(
        functools.partial(_inproj_body, tm=tm, pad=pad, grid_mode=grid_mode),
        grid=(B, T // tm),
        in_specs=x_specs + [vec_spec, vec_spec] + w_specs,
        out_specs=[out_spec] * N_PREP,
        out_shape=[out_shape] * N_PREP,
        scratch_shapes=[pltpu.VMEM((tm + 2 * pad, RW_COLS_PAD), f32)],
        compiler_params=pltpu.CompilerParams(
            dimension_semantics=("parallel", "parallel"), vmem_limit_bytes=VMEM_LIMIT),
        name="inproj_grid" if grid_mode else "inproj_seq",
    )(*x_args, shift, scale, *wts)


def _cumsum3(x, tri):
    x1, x2, x3 = _split3(x)
    return _dot(tri, x1) + _dot(tri, x2) + _dot(tri, x3)


def _lockstep(gens):
    results = [None] * len(gens)
    active = list(enumerate(gens))
    while active:
        still = []
        for i, g in active:
            try:
                next(g)
                still.append((i, g))
            except StopIteration as stop:
                results[i] = stop.value
        active = still
        if active:
            yield
    return results


def _hgrn_chunk(q, v, f, St, c_ref, *, rev, emit, tri):
    C = CHUNK
    k = 1.0 - f
    b = _cumsum3(jnp.log(f), tri)
    yield
    e = 0 if rev else C - 1
    b_end = b[e:e + 1]
    kd = (k * jnp.exp(b_end - b)).astype(bf16)
    vb = v.astype(bf16)
    dS = _dot_tn(vb, kd)
    if not emit:
        yield
        return None, St * jnp.exp(b_end) + dS

    o_inter = _dot_nt((q * jnp.exp(b)).astype(bf16), St.astype(bf16))
    scs = {}
    for i in range(C // SUB):
        r0 = SUB * i
        if not rev and i > 0:
            ref, lo, hi = b[r0 - 1:r0], 0, r0
        elif rev and i < C // SUB - 1:
            ref, lo, hi = b[r0 + SUB:r0 + SUB + 1], r0 + SUB, C
        else:
            continue
        qi = (q[r0:r0 + SUB] * jnp.exp(b[r0:r0 + SUB] - ref)).astype(bf16)
        kj = (k[lo:hi] * jnp.exp(ref - b[lo:hi])).astype(bf16)
        pads = [jnp.zeros((n, LANES), bf16) for n in (lo, C - hi)]
        kj = jnp.concatenate([a for a in (pads[0], kj, pads[1]) if a.shape[0]], 0)
        scs[i] = _dot_nt(qi, kj)
    b2 = b * LOG2E
    c_ref[...] = b2 - jnp.log2(k)
    yield
    row8 = lax.broadcasted_iota(jnp.int32, (8, LANES), 0)
    lane = lax.broadcasted_iota(jnp.int32, (8, C), 1)
    blocks = []
    for i in range(C // SUB):
        r0 = SUB * i
        halves = [scs[i][0:8], scs[i][8:16]] if i in scs else [jnp.zeros((8, C), f32)] * 2
        for sl_ in range(SUB):
            s = r0 + sl_
            hs, sr = divmod(sl_, 8)
            cs = c_ref[s:s + 1, :]
            for h in (0, 1):
                if (h < hs and not rev) or (h > hs and rev):
                    continue
                h0 = r0 + 8 * h
                d = b2[h0:h0 + 8] - cs
                if h == hs:
                    d = jnp.where((row8 <= sr) if rev else (row8 >= sr), d, -jnp.inf)
                w = jnp.sum(q[h0:h0 + 8] * jnp.exp2(d), -1, keepdims=True)
                halves[h] = jnp.where(lane == s, w, halves[h])
        blocks += halves
        yield
    o_intra = _dot(jnp.concatenate(blocks, 0).astype(bf16), vb)
    yield
    return o_inter + o_intra, St * jnp.exp(b_end) + dS


def _tri_inv(L, eye, blk16, blk32):
    Lb = jnp.where(blk16, L, 0.0)
    T = eye + Lb
    Xb = Lb.astype(bf16)
    X2 = _dot(Xb, Xb)
    yield
    for level in range(3):
        Xb = X2.astype(bf16)
        TX = _dot(T.astype(bf16), Xb)
        if level < 2:
            X2 = _dot(Xb, Xb)
        yield
        T = T + TX
    for off_mask in (blk32 & jnp.logical_not(blk16), jnp.logical_not(blk32)):
        off = jnp.where(off_mask, L, 0.0).astype(bf16)
        Tb = T.astype(bf16)
        A = _dot(Tb, off)
        yield
        A = _dot(A.astype(bf16), Tb)
        yield
        T = T + A
    return T


def _rwkv_chunk(r, v, kk, lw, k, bb, Gt, *, rev, emit, tri, masks):
    C = CHUNK
    lane_lo, lo64, pmask, bdmask, eye, blk16, blk32 = masks
    g = _cumsum3(lw, tri)
    yield
    e = 0 if rev else C - 1
    g_end = jnp.exp(g[e:e + 1])
    ieg = jnp.exp(-g)
    at = -kk * jnp.exp(g - lw)
    BK = jnp.concatenate([bb * ieg, k * ieg], 0)
    if emit:
        AR = jnp.concatenate([at, r * jnp.exp(g)], 0).astype(bf16)
    else:
        AR = at.astype(bf16)
    P2 = _dot_nt(AR, Gt.astype(bf16))
    P1s = []
    for h in (0, 1):
        BKh = jnp.where(lane_lo if h == 0 else jnp.logical_not(lane_lo), BK, 0.0).astype(bf16)
        P1s.append(_dot_nt(AR, BKh))
    BKe = (BK * g_end).astype(bf16)
    yield
    vb = v.astype(bf16)
    ZV = jnp.concatenate([jnp.zeros((C, LANES), bf16), vb], 0)
    laks, invs, Mrs = [], [], []
    for h in (0, 1):
        P1 = jnp.where(pmask, P1s[h], 0.0)
        top = P1[0:C]
        laks.append(_dot(top.astype(bf16), ZV))
        invs.append(_tri_inv(top[:, 0:C], eye, blk16, blk32))
        if emit:
            Mrs.append(P1[C:2 * C].astype(bf16))
    Ts = yield from _lockstep(invs)
    Xp = (P2[0:C] + jnp.where(lo64, laks[0], laks[1])).astype(bf16)
    U0, U1 = _dot(Ts[0].astype(bf16), Xp), _dot(Ts[1].astype(bf16), Xp)
    yield
    UV = jnp.concatenate([jnp.where(lo64, U0, U1).astype(bf16), vb], 0)
    dG = _dot_tn(UV, BKe)
    if emit:
        Y0, Y1 = _dot(Mrs[0], UV), _dot(Mrs[1], UV)
    yield
    Gt_new = Gt * g_end + jnp.where(bdmask, dG, 0.0)
    if not emit:
        return None, Gt_new
    return P2[C:2 * C] + jnp.where(lo64, Y0, Y1), Gt_new


def _scan_body(*refs, emit):
    fwd = refs[0:9]
    bwd = refs[9:18]
    hs0_ref, gs0_ref = refs[18:20]
    if emit:
        ho_f, ry_f, ho_b, ry_b, hsT_ref, gsT_ref, hs, gs, crow = refs[20:29]
    else:
        hsT_ref, gsT_ref, hs, gs = refs[20:24]
        crow = None
    j = pl.program_id(1)
    C = CHUNK

    @pl.when(j == 0)
    def _():
        hs[...] = hs0_ref[...]
        gs[...] = gs0_ref[...]

    ri = lax.broadcasted_iota(jnp.int32, (C, C), 0)
    ci = lax.broadcasted_iota(jnp.int32, (C, C), 1)
    tri_f = (ci <= ri).astype(bf16)
    tri_b = (ci >= ri).astype(bf16)
    rows = 2 * C if emit else C
    R = lax.broadcasted_iota(jnp.int32, (rows, 2 * C), 0)
    Cc = lax.broadcasted_iota(jnp.int32, (rows, 2 * C), 1)
    t, s = R & (C - 1), Cc & (C - 1)
    diag_ok = (R >= C) & (s == t)
    pmask_f = (s < t) | diag_ok
    pmask_b = (s > t) | diag_ok
    R2 = lax.broadcasted_iota(jnp.int32, (2 * C, 2 * C), 0)
    C2 = lax.broadcasted_iota(jnp.int32, (2 * C, 2 * C), 1)
    lane_lo = C2 < RW_HEAD
    bdmask = (R2 < RW_HEAD) == lane_lo
    lo64 = lax.broadcasted_iota(jnp.int32, (C, 2 * C), 1) < RW_HEAD
    eye = (ri == ci).astype(f32)
    blk16 = (ri // SUB) == (ci // SUB)
    blk32 = (ri // (2 * SUB)) == (ci // (2 * SUB))
    masks_f = (lane_lo, lo64, pmask_f, bdmask, eye, blk16, blk32)
    masks_b = (lane_lo, lo64, pmask_b, bdmask, eye, blk16, blk32)

    chains, where = [], []
    for bi in range(hs.shape[1]):
        for p in range(NPAIR):
            for d, (src, rev, tri, masks) in enumerate(
                    ((fwd, False, tri_f, masks_f), (bwd, True, tri_b, masks_b))):
                hq, hv, hf, r, v, kk, lw, k, bb = [ref[bi, p] for ref in src]
                c_ref = crow.at[d, bi, p] if emit else None
                chains.append(_hgrn_chunk(hq, hv, hf, hs[d, bi, p], c_ref, rev=rev, emit=emit, tri=tri))
                chains.append(_rwkv_chunk(r, v, kk, lw, k, bb, gs[d, bi, p], rev=rev, emit=emit, tri=tri,
                                          masks=masks))
                where.append((d, bi, p, rev))
    driver = _lockstep(chains)
    try:
        while True:
            next(driver)
    except StopIteration as stop:
        results = stop.value
    for n, (d, bi, p, rev) in enumerate(where):
        (o, St_new), (y, Gt_new) = results[2 * n], results[2 * n + 1]
        hs[d, bi, p] = St_new
        gs[d, bi, p] = Gt_new
        if emit:
            (ho_b if rev else ho_f)[bi, p] = o
            (ry_b if rev else ry_f)[bi, p] = y

    @pl.when(j == pl.num_programs(1) - 1)
    def _():
        hsT_ref[...] = hs[...]
        gsT_ref[...] = gs[...]


def _scan_call(prep, hs0, gs0, *, emit):
    B, _, T, _ = prep["hq"].shape
    n = T // CHUNK
    nb = SCAN_BATCH if B % SCAN_BATCH == 0 else 1
    f_spec = pl.BlockSpec((nb, NPAIR, CHUNK, LANES), lambda b, j: (b, 0, j, 0))
    b_spec = pl.BlockSpec((nb, NPAIR, CHUNK, LANES), lambda b, j: (b, 0, n - 1 - j, 0))
    st_spec = pl.BlockSpec((2, nb, NPAIR, LANES, LANES), lambda b, j: (0, b, 0, 0, 0))
    fwd = [prep[nm] for nm in ("hq", "hv", "hff", "r", "v", "kk", "lwf", "kf", "bf")]
    bwd = [prep[nm] for nm in ("hq", "hv", "hfb", "r", "v", "kk", "lwb", "kb", "bb")]
    seq_shape = jax.ShapeDtypeStruct((B, NPAIR, T, LANES), f32)
    st_shape = jax.ShapeDtypeStruct((2, B, NPAIR, LANES, LANES), f32)
    out_specs = ([f_spec, f_spec, b_spec, b_spec] if emit else []) + [st_spec, st_spec]
    out_shape = ([seq_shape] * 4 if emit else []) + [st_shape, st_shape]
    return pl.pallas_call(
        functools.partial(_scan_body, emit=emit),
        grid=(B // nb, n),
        in_specs=[f_spec] * 9 + [b_spec] * 9 + [st_spec, st_spec],
        out_specs=out_specs,
        out_shape=out_shape,
        scratch_shapes=[pltpu.VMEM((2, nb, NPAIR, LANES, LANES), f32), pltpu.VMEM((2, nb, NPAIR, LANES, LANES), f32)]
        + ([pltpu.VMEM((2, nb, NPAIR, CHUNK, LANES), f32)] if emit else []),
        compiler_params=pltpu.CompilerParams(
            dimension_semantics=("parallel", "arbitrary"), vmem_limit_bytes=VMEM_LIMIT),
        name="scan_latent" if emit else "scan_ctx",
    )(*fwd, *bwd, hs0, gs0)


def _mixout_body(hof, hob, hgate, ryf, ryb, g, bonus, x_ref, g1_ref, wout_ref, hgw_ref, lnx_ref, ln1_ref, o_ref,
                 *, alpha):
    tm = x_ref.shape[1]
    lane_lo = lax.broadcasted_iota(jnp.int32, (tm, LANES), 1) < RW_HEAD
    y_hg, y_rw = [], []
    for p in range(NPAIR):
        o = hof[0, p] + hob[0, p]
        o = o * lax.rsqrt(jnp.mean(o * o, -1, keepdims=True) + HG_NORM_EPS) * hgw_ref[...]
        y_hg.append((o * hgate[0, p]).astype(bf16))
        yr = ryf[0, p] + ryb[0, p]
        m = _seg64_sum(yr, lane_lo) * (1.0 / RW_HEAD)
        yc = yr - m
        var = _seg64_sum(yc * yc, lane_lo) * (1.0 / RW_HEAD)
        sl = slice(p * LANES, (p + 1) * LANES)
        yn = yc * lax.rsqrt(var + RW_GN_EPS) * lnx_ref[0:1, sl] + lnx_ref[1:2, sl]
        y_rw.append(((yn + bonus[0, p]) * g[0, p]).astype(bf16))
    y = jnp.concatenate(y_hg + y_rw, -1)
    mix = _dot(y, wout_ref[...])
    xn = _ln(alpha * x_ref[0] + g1_ref[0] * mix, LN_EPS)
    o_ref[0] = xn * ln1_ref[0:1] + ln1_ref[1:2]


def _mixout_call(scan_outs, prep, x, g1, wout, hgw, lnx, ln1, *, alpha):
    B, T, D = x.shape
    tm = 512
    seq_spec = pl.BlockSpec((1, NPAIR, tm, LANES), lambda b, i: (b, 0, i, 0))
    x_spec = pl.BlockSpec((1, tm, D), lambda b, i: (b, i, 0))
    vec_spec = pl.BlockSpec((1, 1, D), lambda b, i: (b, 0, 0))
    ho_f, ry_f, ho_b, ry_b = scan_outs
    return pl.pallas_call(
        functools.partial(_mixout_body, alpha=alpha),
        grid=(B, T // tm),
        in_specs=[seq_spec] * 7 + [x_spec, vec_spec] + [_const_spec(a.shape) for a in (wout, hgw, lnx, ln1)],
        out_specs=x_spec,
        out_shape=jax.ShapeDtypeStruct((B, T, D), f32),
        compiler_params=pltpu.CompilerParams(
            dimension_semantics=("parallel", "parallel"), vmem_limit_bytes=VMEM_LIMIT),
        name="mix_out",
    )(ho_f, ho_b, prep["hgate"], ry_f, ry_b, prep["g"], prep["bonus"], x, g1, wout, hgw, lnx, ln1)


def _ffn_body(x_ref, sh_ref, sc_ref, g2_ref, wg_ref, wu_ref, wd_ref, ln2_ref, o_ref, *, alpha):
    x1 = x_ref[0]
    u = (_ln(x1, ADA_EPS) * (1.0 + sc_ref[0]) + sh_ref[0]).astype(bf16)
    h = (_silu(_dot(u, wg_ref[...])) * _dot(u, wu_ref[...])).astype(bf16)
    ffn = _dot(h, wd_ref[...])
    xn = _ln(alpha * x1 + g2_ref[0] * ffn, LN_EPS)
    o_ref[0] = xn * ln2_ref[0:1] + ln2_ref[1:2]


def _ffn_call(x1, sh2, sc2, g2, wg, wu, wd, ln2, *, alpha):
    B, T, D = x1.shape
    tm = 512
    x_spec = pl.BlockSpec((1, tm, D), lambda b, i: (b, i, 0))
    vec_spec = pl.BlockSpec((1, 1, D), lambda b, i: (b, 0, 0))
    return pl.pallas_call(
        functools.partial(_ffn_body, alpha=alpha),
        grid=(B, T // tm),
        in_specs=[x_spec, vec_spec, vec_spec, vec_spec] + [_const_spec(a.shape) for a in (wg, wu, wd, ln2)],
        out_specs=x_spec,
        out_shape=jax.ShapeDtypeStruct((B, T, D), f32),
        compiler_params=pltpu.CompilerParams(
            dimension_semantics=("parallel", "parallel"), vmem_limit_bytes=VMEM_LIMIT),
        name="ffn",
    )(x1, sh2, sc2, g2, wg, wu, wd, ln2)


_PREP_NAMES = ("hq", "hv", "hff", "hfb", "hgate", "r", "v", "kk", "lwf", "lwb", "kf", "kb", "bf", "bb", "g", "bonus")


def kernel(x, c, ctx, c_ctx, w_ada, b_ada, w_in, hgrn_lb_logits, hgrn_norm_w, rwkv_mu, rwkv_w0, rwkv_w2, rwkv_a0, rwkv_a2, rwkv_g2, rwkv_k_k, rwkv_k_a, rwkv_r_k, rwkv_lnx_w, rwkv_lnx_b, w_out, ln1_g, ln1_b, w_ffn_gate, w_ffn_up, w_ffn_down, ln2_g, ln2_b):
    B, T, D = x.shape
    depth = w_ada.shape[0]
    assert depth == 1 and T % 256 == 0 and ctx.shape[1] % CHUNK == 0
    alpha = (2.0 * depth) ** 0.25

    lb = jnp.cumsum(jax.nn.softmax(hgrn_lb_logits.astype(f32), axis=0), axis=0)[0]
    w_h = w_in[0, :, :HG_COLS].astype(bf16)
    w_r = jnp.pad(w_in[0, :, HG_COLS:], ((0, 0), (0, RW_COLS_PAD - RW_COLS))).astype(bf16)
    mu = jnp.pad(rwkv_mu[0], (0, RW_COLS_PAD - RW_COLS))[None]
    w_lora = jnp.zeros((LANES, 4 * RW_WIDTH), f32)
    for blk, w in enumerate((rwkv_w2[0, 0], rwkv_w2[0, 1], rwkv_a2[0, 0], rwkv_a2[0, 1])):
        w_lora = w_lora.at[32 * blk:32 * (blk + 1), RW_WIDTH * blk:RW_WIDTH * (blk + 1)].set(w)
    w_lora = w_lora.astype(bf16)
    w0a0 = jnp.concatenate([rwkv_w0[0, 0], rwkv_w0[0, 1], rwkv_a0[0, 0], rwkv_a0[0, 1]])[None]
    g2p = jnp.pad(rwkv_g2[0], ((0, LANES - rwkv_g2.shape[1]), (0, 0))).astype(bf16)
    rwp = jnp.zeros((8, RW_WIDTH), f32).at[0].set(rwkv_k_k[0]).at[1].set(rwkv_k_a[0]).at[2].set(
        rwkv_r_k[0].reshape(-1))
    wts = (w_h, w_r, mu, lb, w_lora, w0a0, g2p, rwp)

    cs = jnp.zeros((16, D), f32).at[:B].set(c).at[B].set(c_ctx)
    mod = _mod_call(cs, w_ada[0], b_ada[0][None])
    sh1, sc1, g1, sh2, sc2, g2 = [m[:B, None, :] for m in jnp.split(mod, 6, axis=-1)]
    ch1, cs1 = [jnp.broadcast_to(m[B:B + 1, None, :], (B, 1, D)) for m in jnp.split(mod, 6, axis=-1)[:2]]

    prep_ctx = dict(zip(_PREP_NAMES, _inproj_call(ctx, ch1, cs1, wts, grid_mode=False)))
    zeros_state = jnp.zeros((2, B, NPAIR, LANES, LANES), f32)
    hs_ctx, gs_ctx = _scan_call(prep_ctx, zeros_state, zeros_state, emit=False)

    prep = dict(zip(_PREP_NAMES, _inproj_call(x, sh1, sc1, wts, grid_mode=True)))
    ho_f, ry_f, ho_b, ry_b, _, _ = _scan_call(prep, hs_ctx, gs_ctx, emit=True)

    lnx = jnp.stack([rwkv_lnx_w[0], rwkv_lnx_b[0]])
    ln1 = jnp.stack([ln1_g[0], ln1_b[0]])
    ln2 = jnp.stack([ln2_g[0], ln2_b[0]])
    x1 = _mixout_call((ho_f, ry_f, ho_b, ry_b), prep, x, g1, w_out[0].astype(bf16), hgrn_norm_w[0][None], lnx, ln1,
                      alpha=alpha)
    return _ffn_call(x1, sh2, sc2, g2, w_ffn_gate[0].astype(bf16), w_ffn_up[0].astype(bf16),
                     w_ffn_down[0].astype(bf16), ln2, alpha=alpha)
```

```python
import functools
import math

import jax
import jax.numpy as jnp
from jax import lax
from jax.experimental import pallas as pl
from jax.experimental.pallas import tpu as pltpu

f32 = jnp.float32
bf16 = jnp.bfloat16

LANES = 128
GRID_W = 64
HG_HEADS = 4
RW_HEADS = 8
RW_HEAD = 64
HG_WIDTH = 512
RW_WIDTH = 512
HG_COLS = 5 * HG_WIDTH
RW_COLS = 1760
RW_COLS_PAD = 1792
LORA_TILE = 12
GLORA_TILE = 13
NPAIR = 4
CHUNK = 64
SUB = 16
SCAN_BATCH = 2
SCAN_STAGGER = 4
ADA_EPS = 1e-6
LN_EPS = 1e-5
HG_NORM_EPS = 1e-5
RW_GN_EPS = 64e-5
EXP_M05 = math.exp(-0.5)
LOG2E = math.log2(math.e)
VMEM_LIMIT = 60 * 1024 * 1024

_NT = (((1,), (1,)), ((), ()))
_TN = (((0,), (0,)), ((), ()))


def _dot(a, b):
    return jnp.dot(a, b, preferred_element_type=f32)


def _dot_nt(a, b):
    return lax.dot_general(a, b, _NT, preferred_element_type=f32)


def _dot_tn(a, b):
    return lax.dot_general(a, b, _TN, preferred_element_type=f32)


def _ln(xf, eps):
    m = jnp.mean(xf, -1, keepdims=True)
    xc = xf - m
    v = jnp.mean(xc * xc, -1, keepdims=True)
    return xc * lax.rsqrt(v + eps)


def _silu(x):
    return x * jax.nn.sigmoid(x)


def _split3(x):
    x1 = x.astype(bf16)
    r1 = x - x1.astype(f32)
    x2 = r1.astype(bf16)
    x3 = (r1 - x2.astype(f32)).astype(bf16)
    return x1, x2, x3


def _seg64_sum(x, lane_lo):
    s_all = jnp.sum(x, -1, keepdims=True)
    s_lo = jnp.sum(jnp.where(lane_lo, x, 0.0), -1, keepdims=True)
    return jnp.where(lane_lo, s_lo, s_all - s_lo)


def _const_spec(shape):
    nd = len(shape)
    return pl.BlockSpec(shape, lambda *_: (0,) * nd, pipeline_mode=pl.Buffered(1))


def _mod_body(cs_ref, w_ref, b_ref, o_ref):
    a = _silu(cs_ref[...])
    a1, a2, _ = _split3(a)
    w1, w2, _ = _split3(w_ref[...])
    o_ref[...] = _dot(a1, w1) + _dot(a1, w2) + _dot(a2, w1) + b_ref[...]


def _mod_call(cs, w_ada, b_ada):
    rows, d = cs.shape
    n = w_ada.shape[1]
    tn = 1024
    return pl.pallas_call(
        _mod_body,
        grid=(n // tn,),
        in_specs=[
            pl.BlockSpec((rows, d), lambda j: (0, 0)),
            pl.BlockSpec((d, tn), lambda j: (0, j)),
            pl.BlockSpec((1, tn), lambda j: (0, j)),
        ],
        out_specs=pl.BlockSpec((rows, tn), lambda j: (0, j)),
        out_shape=jax.ShapeDtypeStruct((rows, n), f32),
        compiler_params=pltpu.CompilerParams(dimension_semantics=("parallel",)),
        name="mod",
    )(cs, w_ada, b_ada)


_PREP_NAMES = ("hq", "hv", "hff", "hfb", "hgate", "r", "v", "kk", "lwf", "lwb", "kf", "kb", "bf", "bb", "g", "bonus")
_PREP_BF16 = ("hgate", "g", "bonus")
N_PREP = len(_PREP_NAMES)


def _inproj_body(*refs, tm, pad, grid_mode):
    if grid_mode:
        x_ref, xu_ref, xd_ref = refs[:3]
        rest = refs[3:]
    else:
        x_ref = refs[0]
        rest = refs[1:]
    (sh_ref, sc_ref, wh_ref, wr_ref, mu_ref, lb_ref, wl_ref, w0a0_ref, g2_ref, rwp_ref) = rest[:10]
    outs = rest[10:10 + N_PREP]
    ext_ref = rest[10 + N_PREP]
    (hq_o, hv_o, hff_o, hfb_o, hgate_o, r_o, v_o, kk_o, lwf_o, lwb_o, kf_o, kb_o, bf_o, bb_o, g_o, bonus_o) = outs

    sh = sh_ref[0]
    sc1 = 1.0 + sc_ref[0]

    def modulate(xv):
        return (_ln(xv, ADA_EPS) * sc1 + sh).astype(bf16)

    u_main = modulate(x_ref[0])
    wr = wr_ref[...]
    if grid_mode:
        i = pl.program_id(1)
        nt = pl.num_programs(1)
        zr_ext = _dot(jnp.concatenate([modulate(xu_ref[0]), u_main, modulate(xd_ref[0])], 0), wr)
        zr_main = zr_ext[pad:pad + tm]
        ext_ref[0:pad] = jnp.where(i > 0, zr_ext[0:pad], 0.0)
        ext_ref[pad + tm:pad + tm + pad] = jnp.where(i < nt - 1, zr_ext[pad + tm:pad + tm + pad], 0.0)
    else:
        zr_main = _dot(u_main, wr)
        ext_ref[0:pad] = jnp.zeros((pad, RW_COLS_PAD), f32)
        ext_ref[pad + tm:pad + tm + pad] = jnp.zeros((pad, RW_COLS_PAD), f32)
    ext_ref[pad:pad + tm] = zr_main
    zh = _dot(u_main, wh_ref[...])

    lane = lax.broadcasted_iota(jnp.int32, (tm, LANES), 1)
    if grid_mode:
        col = lax.broadcasted_iota(jnp.int32, (tm, LANES), 0) & (GRID_W - 1)
        not_first_col = col != 0
        not_last_col = col != GRID_W - 1
        bounds = [0, RW_COLS // 4, RW_COLS // 2, 3 * RW_COLS // 4, RW_COLS]
        windows = [(pad - 1, not_first_col), (pad + 1, not_last_col), (pad - GRID_W, None), (pad + GRID_W, None)]
    else:
        bounds = [0, RW_COLS // 2, RW_COLS]
        windows = [(pad - 1, None), (pad + 1, None)]

    zt = []
    for c in range(RW_COLS_PAD // LANES):
        lo, hi = c * LANES, (c + 1) * LANES
        cs = slice(lo, hi)
        z = zr_main[:, cs]
        shifted = jnp.zeros((tm, LANES), f32)
        for q, (off, valid) in enumerate(windows):
            qlo, qhi = bounds[q], bounds[q + 1]
            if max(lo, qlo) >= min(hi, qhi):
                continue
            win = ext_ref[off:off + tm, cs]
            m = valid
            if not (qlo <= lo and hi <= qhi):
                inq = (lane >= qlo - lo) & (lane < qhi - lo)
                m = inq if m is None else (m & inq)
            shifted = win if m is None else jnp.where(m, win, shifted)
        zt.append(z + mu_ref[:, cs] * (shifted - z))

    lane_lo = lane < RW_HEAD
    lora_in = jnp.where(lane_lo, jnp.tanh(zt[LORA_TILE]), zt[LORA_TILE]).astype(bf16)
    pre = _dot(lora_in, wl_ref[...]) + w0a0_ref[...]
    g_all = _dot(jax.nn.sigmoid(zt[GLORA_TILE]).astype(bf16), g2_ref[...])

    for p in range(NPAIR):
        sl = slice(p * LANES, (p + 1) * LANES)
        r = zt[p]
        k = zt[NPAIR + p]
        v = zt[2 * NPAIR + p]
        lw_f = -jax.nn.sigmoid(pre[:, sl]) * EXP_M05
        lw_b = -jax.nn.sigmoid(pre[:, RW_WIDTH + p * LANES:RW_WIDTH + (p + 1) * LANES]) * EXP_M05
        a_f = jax.nn.sigmoid(pre[:, 2 * RW_WIDTH + p * LANES:2 * RW_WIDTH + (p + 1) * LANES])
        a_b = jax.nn.sigmoid(pre[:, 3 * RW_WIDTH + p * LANES:3 * RW_WIDTH + (p + 1) * LANES])
        kk = k * rwp_ref[0:1, sl]
        kk = kk / jnp.maximum(jnp.sqrt(_seg64_sum(kk * kk, lane_lo)), 1e-12)
        ka = rwp_ref[1:2, sl]
        k_f = k * (1.0 + (a_f - 1.0) * ka)
        k_b = k * (1.0 + (a_b - 1.0) * ka)
        bonus = _seg64_sum(r * rwp_ref[2:3, sl] * (k_f + k_b), lane_lo) * v
        r_o[0, p] = r
        v_o[0, p] = v
        kk_o[0, p] = kk
        lwf_o[0, p] = lw_f
        lwb_o[0, p] = lw_b
        kf_o[0, p] = k_f
        kb_o[0, p] = k_b
        bf_o[0, p] = kk * a_f
        bb_o[0, p] = kk * a_b
        g_o[0, p] = g_all[:, sl].astype(bf16)
        bonus_o[0, p] = bonus.astype(bf16)

    for p in range(NPAIR):
        sl = slice(p * LANES, (p + 1) * LANES)
        lb_f = lb_ref[0:1, sl]
        lb_b = lb_ref[1:2, sl]
        hq_o[0, p] = _silu(zh[:, p * LANES:(p + 1) * LANES])
        hff_o[0, p] = lb_f + (1.0 - lb_f) * jax.nn.sigmoid(zh[:, HG_WIDTH + p * LANES:HG_WIDTH + (p + 1) * LANES])
        hfb_o[0, p] = lb_b + (1.0 - lb_b) * jax.nn.sigmoid(
            zh[:, 2 * HG_WIDTH + p * LANES:2 * HG_WIDTH + (p + 1) * LANES])
        hv_o[0, p] = zh[:, 3 * HG_WIDTH + p * LANES:3 * HG_WIDTH + (p + 1) * LANES]
        hgate_o[0, p] = _silu(zh[:, 4 * HG_WIDTH + p * LANES:4 * HG_WIDTH + (p + 1) * LANES]).astype(bf16)


def _inproj_call(x, shift, scale, wts, *, grid_mode):
    B, T, D = x.shape
    if grid_mode:
        tm, pad = 256, GRID_W
        rows_per_tile = tm // GRID_W
        n_rows = T // GRID_W
        x_specs = [
            pl.BlockSpec((1, tm, D), lambda b, i: (b, i, 0)),
            pl.BlockSpec((1, GRID_W, D), lambda b, i: (b, jnp.maximum(i * rows_per_tile - 1, 0), 0)),
            pl.BlockSpec((1, GRID_W, D), lambda b, i: (b, jnp.minimum((i + 1) * rows_per_tile, n_rows - 1), 0)),
        ]
        x_args = [x, x, x]
    else:
        tm, pad = T, 8
        x_specs = [pl.BlockSpec((1, tm, D), lambda b, i: (b, i, 0))]
        x_args = [x]
    vec_spec = pl.BlockSpec((1, 1, D), lambda b, i: (b, 0, 0))
    w_specs = [_const_spec(w.shape) for w in wts]
    out_spec = pl.BlockSpec((1, NPAIR, tm, LANES), lambda b, i: (b, 0, i, 0))
    out_shape = [jax.ShapeDtypeStruct((B, NPAIR, T, LANES), bf16 if nm in _PREP_BF16 else f32) for nm in _PREP_NAMES]
    return pl.pallas_call(
        functools.partial(_inproj_body, tm=tm, pad=pad, grid_mode=grid_mode),
        grid=(B, T // tm),
        in_specs=x_specs + [vec_spec, vec_spec] + w_specs,
        out_specs=[out_spec] * N_PREP,
        out_shape=out_shape,
        scratch_shapes=[pltpu.VMEM((tm + 2 * pad, RW_COLS_PAD), f32)],
        compiler_params=pltpu.CompilerParams(
            dimension_semantics=("parallel", "parallel"), vmem_limit_bytes=VMEM_LIMIT),
        name="inproj_grid" if grid_mode else "inproj_seq",
    )(*x_args, shift, scale, *wts)


def _cumsum3(x, tri):
    x1, x2, _ = _split3(x)
    return _dot(tri, x1) + _dot(tri, x2)


def _lockstep(gens):
    results = [None] * len(gens)
    active = list(enumerate(gens))
    while active:
        still = []
        for i, g in active:
            try:
                next(g)
                still.append((i, g))
            except StopIteration as stop:
                results[i] = stop.value
        active = still
        if active:
            yield
    return results


def _hgrn_chunk(q, v, f, St, c_ref, *, rev, emit, tri):
    C = CHUNK
    k = 1.0 - f
    b = _cumsum3(jnp.log(f), tri)
    yield
    e = 0 if rev else C - 1
    b_end = b[e:e + 1]
    kd = (k * jnp.exp(b_end - b)).astype(bf16)
    vb = v.astype(bf16)
    dS = _dot_tn(vb, kd)
    if not emit:
        yield
        return None, St * jnp.exp(b_end) + dS

    o_inter = _dot_nt((q * jnp.exp(b)).astype(bf16), St.astype(bf16))
    scs = {}
    for i in range(C // SUB):
        r0 = SUB * i
        if not rev and i > 0:
            ref, lo, hi = b[r0 - 1:r0], 0, r0
        elif rev and i < C // SUB - 1:
            ref, lo, hi = b[r0 + SUB:r0 + SUB + 1], r0 + SUB, C
        else:
            continue
        qi = (q[r0:r0 + SUB] * jnp.exp(b[r0:r0 + SUB] - ref)).astype(bf16)
        kj = (k[lo:hi] * jnp.exp(ref - b[lo:hi])).astype(bf16)
        pads = [jnp.zeros((n, LANES), bf16) for n in (lo, C - hi)]
        kj = jnp.concatenate([a for a in (pads[0], kj, pads[1]) if a.shape[0]], 0)
        scs[i] = _dot_nt(qi, kj)
    b2 = b * LOG2E
    c_ref[...] = b2 - jnp.log2(k)
    yield
    row8 = lax.broadcasted_iota(jnp.int32, (8, LANES), 0)
    lane = lax.broadcasted_iota(jnp.int32, (8, C), 1)
    blocks = []
    for i in range(C // SUB):
        r0 = SUB * i
        halves = [scs[i][0:8], scs[i][8:16]] if i in scs else [jnp.zeros((8, C), f32)] * 2
        for sl_ in range(SUB):
            s = r0 + sl_
            hs, sr = divmod(sl_, 8)
            cs = c_ref[s:s + 1, :]
            for h in (0, 1):
                if (h < hs and not rev) or (h > hs and rev):
                    continue
                h0 = r0 + 8 * h
                d = b2[h0:h0 + 8] - cs
                if h == hs:
                    d = jnp.where((row8 <= sr) if rev else (row8 >= sr), d, -jnp.inf)
                w = jnp.sum(q[h0:h0 + 8] * jnp.exp2(d), -1, keepdims=True)
                halves[h] = jnp.where(lane == s, w, halves[h])
            if sl_ % 8 == 7:
                yield
        blocks += halves
    o_intra = _dot(jnp.concatenate(blocks, 0).astype(bf16), vb)
    yield
    return o_inter + o_intra, St * jnp.exp(b_end) + dS


def _bd(x, lo):
    zero = jnp.zeros_like(x)
    return jnp.concatenate([jnp.where(lo, x, zero), jnp.where(lo, zero, x)], 0)


def _tri_inv(L, lo, eye, blk16, blk32):
    Lb = jnp.where(blk16, L, 0.0)
    T = eye + Lb
    Xb = Lb.astype(bf16)
    X2 = _dot(Xb, _bd(Xb, lo))
    yield
    for level in range(3):
        Xb = X2.astype(bf16)
        Xbd = _bd(Xb, lo)
        TX = _dot(T.astype(bf16), Xbd)
        if level < 2:
            X2 = _dot(Xb, Xbd)
        yield
        T = T + TX
    for off_mask in (blk32 & jnp.logical_not(blk16), jnp.logical_not(blk32)):
        off = jnp.where(off_mask, L, 0.0).astype(bf16)
        Tb = T.astype(bf16)
        A = _dot(Tb, _bd(off, lo))
        yield
        A = _dot(A.astype(bf16), _bd(Tb, lo))
        yield
        T = T + A
    return T


def _rwkv_chunk(r, v, kk, lw, k, bb, Gt, *, rev, emit, tri, masks):
    C = CHUNK
    lo, pmask, bdmask, eye, blk16, blk32 = masks
    g = _cumsum3(lw, tri)
    yield
    e = 0 if rev else C - 1
    g_end = jnp.exp(g[e:e + 1])
    ieg = jnp.exp(-g)
    at = -kk * jnp.exp(g - lw)
    bt, kt = bb * ieg, k * ieg
    if emit:
        AR = jnp.concatenate([at, r * jnp.exp(g)], 0).astype(bf16)
    else:
        AR = at.astype(bf16)
    P2 = _dot_nt(AR, Gt.astype(bf16))
    P1 = _dot_nt(AR, jnp.concatenate([_bd(bt, lo), _bd(kt, lo)], 0).astype(bf16))
    BKe = (jnp.concatenate([bt, kt], 0) * g_end).astype(bf16)
    yield
    vb = v.astype(bf16)
    vbd = _bd(vb, lo)
    P1 = jnp.where(pmask, P1, 0.0)
    lak = _dot(P1[0:C, 2 * C:4 * C].astype(bf16), vbd)
    if emit:
        Mr = P1[C:2 * C].astype(bf16)
    T = yield from _tri_inv(P1[0:C, 0:2 * C], lo, eye, blk16, blk32)
    Xp = (P2[0:C] + lak).astype(bf16)
    U = _dot(T.astype(bf16), _bd(Xp, lo))
    yield
    Ub = U.astype(bf16)
    dG = _dot_tn(jnp.concatenate([Ub, vb], 0), BKe)
    if emit:
        Y = _dot(Mr, jnp.concatenate([_bd(Ub, lo), vbd], 0))
    yield
    Gt_new = Gt * g_end + jnp.where(bdmask, dG, 0.0)
    if not emit:
        return None, Gt_new
    return P2[C:2 * C] + Y, Gt_new


def _scan_body(*refs, emit):
    fwd = refs[0:9]
    bwd = refs[9:18]
    hs0_ref, gs0_ref = refs[18:20]
    if emit:
        ho_f, ry_f, ho_b, ry_b, hsT_ref, gsT_ref, hs, gs, crow = refs[20:29]
    else:
        hsT_ref, gsT_ref, hs, gs = refs[20:24]
        crow = None
    j = pl.program_id(1)
    C = CHUNK

    @pl.when(j == 0)
    def _():
        hs[...] = hs0_ref[...]
        gs[...] = gs0_ref[...]

    ri = lax.broadcasted_iota(jnp.int32, (C, C), 0)
    ci = lax.broadcasted_iota(jnp.int32, (C, C), 1)
    tri_f = (ci <= ri).astype(bf16)
    tri_b = (ci >= ri).astype(bf16)
    rows = 2 * C if emit else C
    R = lax.broadcasted_iota(jnp.int32, (rows, 4 * C), 0)
    Cc = lax.broadcasted_iota(jnp.int32, (rows, 4 * C), 1)
    t, s = R & (C - 1), Cc & (C - 1)
    diag_ok = (R >= C) & (s == t)
    pmask_f = (s < t) | diag_ok
    pmask_b = (s > t) | diag_ok
    R2 = lax.broadcasted_iota(jnp.int32, (2 * C, 2 * C), 0)
    C2 = lax.broadcasted_iota(jnp.int32, (2 * C, 2 * C), 1)
    bdmask = (R2 < RW_HEAD) == (C2 < RW_HEAD)
    rp = lax.broadcasted_iota(jnp.int32, (C, 2 * C), 0)
    lp = lax.broadcasted_iota(jnp.int32, (C, 2 * C), 1)
    sp = lp & (C - 1)
    lo = lp < RW_HEAD
    eye = (rp == sp).astype(f32)
    blk16 = (rp // SUB) == (sp // SUB)
    blk32 = (rp // (2 * SUB)) == (sp // (2 * SUB))
    masks_f = (lo, pmask_f, bdmask, eye, blk16, blk32)
    masks_b = (lo, pmask_b, bdmask, eye, blk16, blk32)

    rw_chains, hg_chains, where = [], [], []
    for bi in range(hs.shape[1]):
        for p in range(NPAIR):
            for d, (src, rev, tri, masks) in enumerate(
                    ((fwd, False, tri_f, masks_f), (bwd, True, tri_b, masks_b))):
                hq, hv, hf, r, v, kk, lw, k, bb = [ref[bi, p] for ref in src]
                c_ref = crow.at[d, bi, p] if emit else None
                hg_chains.append(_hgrn_chunk(hq, hv, hf, hs[d, bi, p], c_ref, rev=rev, emit=emit, tri=tri))
                rw_chains.append(_rwkv_chunk(r, v, kk, lw, k, bb, gs[d, bi, p], rev=rev, emit=emit, tri=tri,
                                             masks=masks))
                where.append((d, bi, p, rev))
    def delayed(gen, rounds):
        for _ in range(rounds):
            yield
        return (yield from gen)

    per_elem = 2 * NPAIR
    order = rw_chains + hg_chains
    order = [delayed(g, SCAN_STAGGER * ((n % len(where)) // per_elem)) for n, g in enumerate(order)]
    driver = _lockstep(order)
    try:
        while True:
            next(driver)
    except StopIteration as stop:
        results = stop.value
    for n, (d, bi, p, rev) in enumerate(where):
        (y, Gt_new), (o, St_new) = results[n], results[len(where) + n]
        hs[d, bi, p] = St_new
        gs[d, bi, p] = Gt_new
        if emit:
            (ho_b if rev else ho_f)[bi, p] = o.astype(bf16)
            (ry_b if rev else ry_f)[bi, p] = y.astype(bf16)

    @pl.when(j == pl.num_programs(1) - 1)
    def _():
        hsT_ref[...] = hs[...]
        gsT_ref[...] = gs[...]


def _scan_call(prep, hs0, gs0, *, emit):
    B, _, T, _ = prep["hq"].shape
    n = T // CHUNK
    nb = SCAN_BATCH if B % SCAN_BATCH == 0 else 1
    f_spec = pl.BlockSpec((nb, NPAIR, CHUNK, LANES), lambda b, j: (b, 0, j, 0))
    b_spec = pl.BlockSpec((nb, NPAIR, CHUNK, LANES), lambda b, j: (b, 0, n - 1 - j, 0))
    st_spec = pl.BlockSpec((2, nb, NPAIR, LANES, LANES), lambda b, j: (0, b, 0, 0, 0))
    fwd = [prep[nm] for nm in ("hq", "hv", "hff", "r", "v", "kk", "lwf", "kf", "bf")]
    bwd = [prep[nm] for nm in ("hq", "hv", "hfb", "r", "v", "kk", "lwb", "kb", "bb")]
    seq_shape = jax.ShapeDtypeStruct((B, NPAIR, T, LANES), bf16)
    st_shape = jax.ShapeDtypeStruct((2, B, NPAIR, LANES, LANES), f32)
    out_specs = ([f_spec, f_spec, b_spec, b_spec] if emit else []) + [st_spec, st_spec]
    out_shape = ([seq_shape] * 4 if emit else []) + [st_shape, st_shape]
    return pl.pallas_call(
        functools.partial(_scan_body, emit=emit),
        grid=(B // nb, n),
        in_specs=[f_spec] * 9 + [b_spec] * 9 + [st_spec, st_spec],
        out_specs=out_specs,
        out_shape=out_shape,
        scratch_shapes=[pltpu.VMEM((2, nb, NPAIR, LANES, LANES), f32), pltpu.VMEM((2, nb, NPAIR, LANES, LANES), f32)]
        + ([pltpu.VMEM((2, nb, NPAIR, CHUNK, LANES), f32)] if emit else []),
        compiler_params=pltpu.CompilerParams(
            dimension_semantics=("parallel", "arbitrary"), vmem_limit_bytes=VMEM_LIMIT),
        name="scan_latent" if emit else "scan_ctx",
    )(*fwd, *bwd, hs0, gs0)


def _mixout_body(hof, hob, hgate, ryf, ryb, g, bonus, x_ref, g1_ref, wout_ref, hgw_ref, lnx_ref, ln1_ref, o_ref,
                 *, alpha):
    tm = x_ref.shape[1]
    lane_lo = lax.broadcasted_iota(jnp.int32, (tm, LANES), 1) < RW_HEAD
    y_hg, y_rw = [], []
    for p in range(NPAIR):
        o = hof[0, p].astype(f32) + hob[0, p].astype(f32)
        o = o * lax.rsqrt(jnp.mean(o * o, -1, keepdims=True) + HG_NORM_EPS) * hgw_ref[...]
        y_hg.append((o * hgate[0, p].astype(f32)).astype(bf16))
        yr = ryf[0, p].astype(f32) + ryb[0, p].astype(f32)
        m = _seg64_sum(yr, lane_lo) * (1.0 / RW_HEAD)
        yc = yr - m
        var = _seg64_sum(yc * yc, lane_lo) * (1.0 / RW_HEAD)
        sl = slice(p * LANES, (p + 1) * LANES)
        yn = yc * lax.rsqrt(var + RW_GN_EPS) * lnx_ref[0:1, sl] + lnx_ref[1:2, sl]
        y_rw.append(((yn + bonus[0, p].astype(f32)) * g[0, p].astype(f32)).astype(bf16))
    y = jnp.concatenate(y_hg + y_rw, -1)
    mix = _dot(y, wout_ref[...])
    xn = _ln(alpha * x_ref[0] + g1_ref[0] * mix, LN_EPS)
    o_ref[0] = xn * ln1_ref[0:1] + ln1_ref[1:2]


def _mixout_call(scan_outs, prep, x, g1, wout, hgw, lnx, ln1, *, alpha):
    B, T, D = x.shape
    tm = 512
    seq_spec = pl.BlockSpec((1, NPAIR, tm, LANES), lambda b, i: (b, 0, i, 0))
    x_spec = pl.BlockSpec((1, tm, D), lambda b, i: (b, i, 0))
    vec_spec = pl.BlockSpec((1, 1, D), lambda b, i: (b, 0, 0))
    ho_f, ry_f, ho_b, ry_b = scan_outs
    return pl.pallas_call(
        functools.partial(_mixout_body, alpha=alpha),
        grid=(B, T // tm),
        in_specs=[seq_spec] * 7 + [x_spec, vec_spec] + [_const_spec(a.shape) for a in (wout, hgw, lnx, ln1)],
        out_specs=x_spec,
        out_shape=jax.ShapeDtypeStruct((B, T, D), f32),
        compiler_params=pltpu.CompilerParams(
            dimension_semantics=("parallel", "parallel"), vmem_limit_bytes=VMEM_LIMIT),
        name="mix_out",
    )(ho_f, ho_b, prep["hgate"], ry_f, ry_b, prep["g"], prep["bonus"], x, g1, wout, hgw, lnx, ln1)


def _ffn_body(x_ref, sh_ref, sc_ref, g2_ref, wg_ref, wu_ref, wd_ref, ln2_ref, o_ref, *, alpha):
    x1 = x_ref[0]
    u = (_ln(x1, ADA_EPS) * (1.0 + sc_ref[0]) + sh_ref[0]).astype(bf16)
    h = (_silu(_dot(u, wg_ref[...])) * _dot(u, wu_ref[...])).astype(bf16)
    ffn = _dot(h, wd_ref[...])
    xn = _ln(alpha * x1 + g2_ref[0] * ffn, LN_EPS)
    o_ref[0] = xn * ln2_ref[0:1] + ln2_ref[1:2]


def _ffn_call(x1, sh2, sc2, g2, wg, wu, wd, ln2, *, alpha):
    B, T, D = x1.shape
    tm = 512
    x_spec = pl.BlockSpec((1, tm, D), lambda b, i: (b, i, 0))
    vec_spec = pl.BlockSpec((1, 1, D), lambda b, i: (b, 0, 0))
    return pl.pallas_call(
        functools.partial(_ffn_body, alpha=alpha),
        grid=(B, T // tm),
        in_specs=[x_spec, vec_spec, vec_spec, vec_spec] + [_const_spec(a.shape) for a in (wg, wu, wd, ln2)],
        out_specs=x_spec,
        out_shape=jax.ShapeDtypeStruct((B, T, D), f32),
        compiler_params=pltpu.CompilerParams(
            dimension_semantics=("parallel", "parallel"), vmem_limit_bytes=VMEM_LIMIT),
        name="ffn",
    )(x1, sh2, sc2, g2, wg, wu, wd, ln2)


def kernel(x, c, ctx, c_ctx, w_ada, b_ada, w_in, hgrn_lb_logits, hgrn_norm_w, rwkv_mu, rwkv_w0, rwkv_w2, rwkv_a0, rwkv_a2, rwkv_g2, rwkv_k_k, rwkv_k_a, rwkv_r_k, rwkv_lnx_w, rwkv_lnx_b, w_out, ln1_g, ln1_b, w_ffn_gate, w_ffn_up, w_ffn_down, ln2_g, ln2_b):
    B, T, D = x.shape
    depth = w_ada.shape[0]
    assert depth == 1 and T % 256 == 0 and ctx.shape[1] % CHUNK == 0
    alpha = (2.0 * depth) ** 0.25

    lb = jnp.cumsum(jax.nn.softmax(hgrn_lb_logits.astype(f32), axis=0), axis=0)[0]
    w_h = w_in[0, :, :HG_COLS].astype(bf16)
    w_r = jnp.pad(w_in[0, :, HG_COLS:], ((0, 0), (0, RW_COLS_PAD - RW_COLS))).astype(bf16)
    mu = jnp.pad(rwkv_mu[0], (0, RW_COLS_PAD - RW_COLS))[None]
    w_lora = jnp.zeros((LANES, 4 * RW_WIDTH), f32)
    for blk, w in enumerate((rwkv_w2[0, 0], rwkv_w2[0, 1], rwkv_a2[0, 0], rwkv_a2[0, 1])):
        w_lora = w_lora.at[32 * blk:32 * (blk + 1), RW_WIDTH * blk:RW_WIDTH * (blk + 1)].set(w)
    w_lora = w_lora.astype(bf16)
    w0a0 = jnp.concatenate([rwkv_w0[0, 0], rwkv_w0[0, 1], rwkv_a0[0, 0], rwkv_a0[0, 1]])[None]
    g2p = jnp.pad(rwkv_g2[0], ((0, LANES - rwkv_g2.shape[1]), (0, 0))).astype(bf16)
    rwp = jnp.zeros((8, RW_WIDTH), f32).at[0].set(rwkv_k_k[0]).at[1].set(rwkv_k_a[0]).at[2].set(
        rwkv_r_k[0].reshape(-1))
    wts = (w_h, w_r, mu, lb, w_lora, w0a0, g2p, rwp)

    cs = jnp.zeros((16, D), f32).at[:B].set(c).at[B].set(c_ctx)
    mod = _mod_call(cs, w_ada[0], b_ada[0][None])
    sh1, sc1, g1, sh2, sc2, g2 = [m[:B, None, :] for m in jnp.split(mod, 6, axis=-1)]
    ch1, cs1 = [jnp.broadcast_to(m[B:B + 1, None, :], (B, 1, D)) for m in jnp.split(mod, 6, axis=-1)[:2]]

    prep_ctx = dict(zip(_PREP_NAMES, _inproj_call(ctx, ch1, cs1, wts, grid_mode=False)))
    zeros_state = jnp.zeros((2, B, NPAIR, LANES, LANES), f32)
    hs_ctx, gs_ctx = _scan_call(prep_ctx, zeros_state, zeros_state, emit=False)

    prep = dict(zip(_PREP_NAMES, _inproj_call(x, sh1, sc1, wts, grid_mode=True)))
    ho_f, ry_f, ho_b, ry_b, _, _ = _scan_call(prep, hs_ctx, gs_ctx, emit=True)

    lnx = jnp.stack([rwkv_lnx_w[0], rwkv_lnx_b[0]])
    ln1 = jnp.stack([ln1_g[0], ln1_b[0]])
    ln2 = jnp.stack([ln2_g[0], ln2_b[0]])
    x1 = _mixout_call((ho_f, ry_f, ho_b, ry_b), prep, x, g1, w_out[0].astype(bf16), hgrn_norm_w[0][None], lnx, ln1,
                      alpha=alpha)
    return _ffn_call(x1, sh2, sc2, g2, w_ffn_gate[0].astype(bf16), w_ffn_up[0].astype(bf16),
                     w_ffn_down[0].astype(bf16), ln2, alpha=alpha)
```

```python
import functools
import math

import jax
import jax.numpy as jnp
from jax import lax
from jax.experimental import pallas as pl
from jax.experimental.pallas import tpu as pltpu

f32 = jnp.float32
bf16 = jnp.bfloat16

LANES = 128
GRID_W = 64
HG_HEADS = 4
RW_HEADS = 8
RW_HEAD = 64
HG_WIDTH = 512
RW_WIDTH = 512
HG_COLS = 5 * HG_WIDTH
RW_COLS = 1760
RW_COLS_PAD = 1792
LORA_TILE = 12
GLORA_TILE = 13
NPAIR = 4
CHUNK = 64
SUB = 16
SCAN_BATCH = 4
TAIL_TM = 512
TAIL_SPLIT = 2
SCAN_STAGGER = 4
ADA_EPS = 1e-6
LN_EPS = 1e-5
HG_NORM_EPS = 1e-5
RW_GN_EPS = 64e-5
EXP_M05 = math.exp(-0.5)
LOG2E = math.log2(math.e)
VMEM_LIMIT = 60 * 1024 * 1024

_NT = (((1,), (1,)), ((), ()))
_TN = (((0,), (0,)), ((), ()))


def _dot(a, b):
    return jnp.dot(a, b, preferred_element_type=f32)


def _dot_nt(a, b):
    return lax.dot_general(a, b, _NT, preferred_element_type=f32)


def _dot_tn(a, b):
    return lax.dot_general(a, b, _TN, preferred_element_type=f32)


def _ln(xf, eps):
    m = jnp.mean(xf, -1, keepdims=True)
    xc = xf - m
    v = jnp.mean(xc * xc, -1, keepdims=True)
    return xc * lax.rsqrt(v + eps)


def _silu(x):
    return x * jax.nn.sigmoid(x)


def _split3(x):
    x1 = x.astype(bf16)
    r1 = x - x1.astype(f32)
    x2 = r1.astype(bf16)
    x3 = (r1 - x2.astype(f32)).astype(bf16)
    return x1, x2, x3


def _seg64_sum(x, lane_lo):
    s_all = jnp.sum(x, -1, keepdims=True)
    s_lo = jnp.sum(jnp.where(lane_lo, x, 0.0), -1, keepdims=True)
    return jnp.where(lane_lo, s_lo, s_all - s_lo)


def _const_spec(shape):
    nd = len(shape)
    return pl.BlockSpec(shape, lambda *_: (0,) * nd, pipeline_mode=pl.Buffered(1))


def _mod_body(cs_ref, w_ref, b_ref, o_ref):
    a = _silu(cs_ref[...])
    a1, a2, _ = _split3(a)
    w1, w2, _ = _split3(w_ref[...])
    o_ref[...] = _dot(a1, w1) + _dot(a1, w2) + _dot(a2, w1) + b_ref[...]


def _mod_call(cs, w_ada, b_ada):
    rows, d = cs.shape
    n = w_ada.shape[1]
    tn = 1024
    return pl.pallas_call(
        _mod_body,
        grid=(n // tn,),
        in_specs=[
            pl.BlockSpec((rows, d), lambda j: (0, 0)),
            pl.BlockSpec((d, tn), lambda j: (0, j)),
            pl.BlockSpec((1, tn), lambda j: (0, j)),
        ],
        out_specs=pl.BlockSpec((rows, tn), lambda j: (0, j)),
        out_shape=jax.ShapeDtypeStruct((rows, n), f32),
        compiler_params=pltpu.CompilerParams(dimension_semantics=("parallel",)),
        name="mod",
    )(cs, w_ada, b_ada)


_PREP_NAMES = ("hq", "hv", "hff", "hfb", "hgate", "r", "v", "kk", "lwf", "lwb", "kf", "kb", "bf", "bb", "g", "bonus")
_PREP_BF16 = ("hgate", "g", "bonus")
N_PREP = len(_PREP_NAMES)


def _inproj_body(*refs, tm, pad, grid_mode):
    if grid_mode:
        x_ref, xu_ref, xd_ref = refs[:3]
        rest = refs[3:]
    else:
        x_ref = refs[0]
        rest = refs[1:]
    (sh_ref, sc_ref, wh_ref, wr_ref, mu_ref, lb_ref, wl_ref, w0a0_ref, g2_ref, rwp_ref) = rest[:10]
    outs = rest[10:10 + N_PREP]
    ext_ref = rest[10 + N_PREP]
    (hq_o, hv_o, hff_o, hfb_o, hgate_o, r_o, v_o, kk_o, lwf_o, lwb_o, kf_o, kb_o, bf_o, bb_o, g_o, bonus_o) = outs

    sh = sh_ref[0]
    sc1 = 1.0 + sc_ref[0]

    def modulate(xv):
        return (_ln(xv, ADA_EPS) * sc1 + sh).astype(bf16)

    u_main = modulate(x_ref[0])
    wr = wr_ref[...]
    if grid_mode:
        i = pl.program_id(1)
        nt = pl.num_programs(1)
        zr_ext = _dot(jnp.concatenate([modulate(xu_ref[0]), u_main, modulate(xd_ref[0])], 0), wr)
        zr_main = zr_ext[pad:pad + tm]
        ext_ref[0:pad] = jnp.where(i > 0, zr_ext[0:pad], 0.0)
        ext_ref[pad + tm:pad + tm + pad] = jnp.where(i < nt - 1, zr_ext[pad + tm:pad + tm + pad], 0.0)
    else:
        zr_main = _dot(u_main, wr)
        ext_ref[0:pad] = jnp.zeros((pad, RW_COLS_PAD), f32)
        ext_ref[pad + tm:pad + tm + pad] = jnp.zeros((pad, RW_COLS_PAD), f32)
    ext_ref[pad:pad + tm] = zr_main
    hg_split = 2 * HG_WIDTH
    zh_a = _dot(u_main, wh_ref[:, 0:hg_split])

    lane = lax.broadcasted_iota(jnp.int32, (tm, LANES), 1)
    if grid_mode:
        col = lax.broadcasted_iota(jnp.int32, (tm, LANES), 0) & (GRID_W - 1)
        not_first_col = col != 0
        not_last_col = col != GRID_W - 1
        bounds = [0, RW_COLS // 4, RW_COLS // 2, 3 * RW_COLS // 4, RW_COLS]
        windows = [(pad - 1, not_first_col), (pad + 1, not_last_col), (pad - GRID_W, None), (pad + GRID_W, None)]
    else:
        bounds = [0, RW_COLS // 2, RW_COLS]
        windows = [(pad - 1, None), (pad + 1, None)]

    zt = []
    for c in range(RW_COLS_PAD // LANES):
        lo, hi = c * LANES, (c + 1) * LANES
        cs = slice(lo, hi)
        z = zr_main[:, cs]
        shifted = jnp.zeros((tm, LANES), f32)
        for q, (off, valid) in enumerate(windows):
            qlo, qhi = bounds[q], bounds[q + 1]
            if max(lo, qlo) >= min(hi, qhi):
                continue
            win = ext_ref[off:off + tm, cs]
            m = valid
            if not (qlo <= lo and hi <= qhi):
                inq = (lane >= qlo - lo) & (lane < qhi - lo)
                m = inq if m is None else (m & inq)
            shifted = win if m is None else jnp.where(m, win, shifted)
        zt.append(z + mu_ref[:, cs] * (shifted - z))

    lane_lo = lane < RW_HEAD
    lora_in = jnp.where(lane_lo, jnp.tanh(zt[LORA_TILE]), zt[LORA_TILE]).astype(bf16)
    pre = _dot(lora_in, wl_ref[...]) + w0a0_ref[...]
    g_all = _dot(jax.nn.sigmoid(zt[GLORA_TILE]).astype(bf16), g2_ref[...])
    zh = jnp.concatenate([zh_a, _dot(u_main, wh_ref[:, hg_split:HG_COLS])], -1)

    for p in range(NPAIR):
        sl = slice(p * LANES, (p + 1) * LANES)
        r = zt[p]
        k = zt[NPAIR + p]
        v = zt[2 * NPAIR + p]
        lw_f = -jax.nn.sigmoid(pre[:, sl]) * EXP_M05
        lw_b = -jax.nn.sigmoid(pre[:, RW_WIDTH + p * LANES:RW_WIDTH + (p + 1) * LANES]) * EXP_M05
        a_f = jax.nn.sigmoid(pre[:, 2 * RW_WIDTH + p * LANES:2 * RW_WIDTH + (p + 1) * LANES])
        a_b = jax.nn.sigmoid(pre[:, 3 * RW_WIDTH + p * LANES:3 * RW_WIDTH + (p + 1) * LANES])
        kk = k * rwp_ref[0:1, sl]
        kk = kk / jnp.maximum(jnp.sqrt(_seg64_sum(kk * kk, lane_lo)), 1e-12)
        ka = rwp_ref[1:2, sl]
        k_f = k * (1.0 + (a_f - 1.0) * ka)
        k_b = k * (1.0 + (a_b - 1.0) * ka)
        bonus = _seg64_sum(r * rwp_ref[2:3, sl] * (k_f + k_b), lane_lo) * v
        r_o[0, p] = r
        v_o[0, p] = v
        kk_o[0, p] = kk
        lwf_o[0, p] = lw_f
        lwb_o[0, p] = lw_b
        kf_o[0, p] = k_f
        kb_o[0, p] = k_b
        bf_o[0, p] = kk * a_f
        bb_o[0, p] = kk * a_b
        g_o[0, p] = g_all[:, sl].astype(bf16)
        bonus_o[0, p] = bonus.astype(bf16)

    for p in range(NPAIR):
        sl = slice(p * LANES, (p + 1) * LANES)
        lb_f = lb_ref[0:1, sl]
        lb_b = lb_ref[1:2, sl]
        hq_o[0, p] = _silu(zh[:, p * LANES:(p + 1) * LANES])
        hff_o[0, p] = lb_f + (1.0 - lb_f) * jax.nn.sigmoid(zh[:, HG_WIDTH + p * LANES:HG_WIDTH + (p + 1) * LANES])
        hfb_o[0, p] = lb_b + (1.0 - lb_b) * jax.nn.sigmoid(
            zh[:, 2 * HG_WIDTH + p * LANES:2 * HG_WIDTH + (p + 1) * LANES])
        hv_o[0, p] = zh[:, 3 * HG_WIDTH + p * LANES:3 * HG_WIDTH + (p + 1) * LANES]
        hgate_o[0, p] = _silu(zh[:, 4 * HG_WIDTH + p * LANES:4 * HG_WIDTH + (p + 1) * LANES]).astype(bf16)


def _inproj_call(x, shift, scale, wts, *, grid_mode):
    B, T, D = x.shape
    if grid_mode:
        tm, pad = 256, GRID_W
        rows_per_tile = tm // GRID_W
        n_rows = T // GRID_W
        x_specs = [
            pl.BlockSpec((1, tm, D), lambda b, i: (b, i, 0)),
            pl.BlockSpec((1, GRID_W, D), lambda b, i: (b, jnp.maximum(i * rows_per_tile - 1, 0), 0)),
            pl.BlockSpec((1, GRID_W, D), lambda b, i: (b, jnp.minimum((i + 1) * rows_per_tile, n_rows - 1), 0)),
        ]
        x_args = [x, x, x]
    else:
        tm, pad = T, 8
        x_specs = [pl.BlockSpec((1, tm, D), lambda b, i: (b, i, 0))]
        x_args = [x]
    vec_spec = pl.BlockSpec((1, 1, D), lambda b, i: (b, 0, 0))
    w_specs = [_const_spec(w.shape) for w in wts]
    out_spec = pl.BlockSpec((1, NPAIR, tm, LANES), lambda b, i: (b, 0, i, 0))
    out_shape = [jax.ShapeDtypeStruct((B, NPAIR, T, LANES), bf16 if nm in _PREP_BF16 else f32) for nm in _PREP_NAMES]
    return pl.pallas_call(
        functools.partial(_inproj_body, tm=tm, pad=pad, grid_mode=grid_mode),
        grid=(B, T // tm),
        in_specs=x_specs + [vec_spec, vec_spec] + w_specs,
        out_specs=[out_spec] * N_PREP,
        out_shape=out_shape,
        scratch_shapes=[pltpu.VMEM((tm + 2 * pad, RW_COLS_PAD), f32)],
        compiler_params=pltpu.CompilerParams(
            dimension_semantics=("parallel", "parallel"), vmem_limit_bytes=VMEM_LIMIT),
        name="inproj_grid" if grid_mode else "inproj_seq",
    )(*x_args, shift, scale, *wts)


def _cumsum3(x, tri):
    x1, x2, _ = _split3(x)
    return _dot(tri, x1) + _dot(tri, x2)


def _lockstep(gens):
    results = [None] * len(gens)
    active = list(enumerate(gens))
    while active:
        still = []
        for i, g in active:
            try:
                next(g)
                still.append((i, g))
            except StopIteration as stop:
                results[i] = stop.value
        active = still
        if active:
            yield
    return results


def _hgrn_chunk(q, v, f, St, c_ref, *, rev, emit, tri):
    C = CHUNK
    k = 1.0 - f
    b = _cumsum3(jnp.log(f), tri)
    yield
    e = 0 if rev else C - 1
    b_end = b[e:e + 1]
    kd = (k * jnp.exp(b_end - b)).astype(bf16)
    vb = v.astype(bf16)
    dS = _dot_tn(vb, kd)
    if not emit:
        yield
        return None, St * jnp.exp(b_end) + dS

    o_inter = _dot_nt((q * jnp.exp(b)).astype(bf16), St.astype(bf16))
    scs = {}
    for i in range(C // SUB):
        r0 = SUB * i
        if not rev and i > 0:
            ref, lo, hi = b[r0 - 1:r0], 0, r0
        elif rev and i < C // SUB - 1:
            ref, lo, hi = b[r0 + SUB:r0 + SUB + 1], r0 + SUB, C
        else:
            continue
        qi = (q[r0:r0 + SUB] * jnp.exp(b[r0:r0 + SUB] - ref)).astype(bf16)
        kj = (k[lo:hi] * jnp.exp(ref - b[lo:hi])).astype(bf16)
        pads = [jnp.zeros((n, LANES), bf16) for n in (lo, C - hi)]
        kj = jnp.concatenate([a for a in (pads[0], kj, pads[1]) if a.shape[0]], 0)
        scs[i] = _dot_nt(qi, kj)
    b2 = b * LOG2E
    c_ref[...] = b2 - jnp.log2(k)
    yield
    row8 = lax.broadcasted_iota(jnp.int32, (8, LANES), 0)
    lane = lax.broadcasted_iota(jnp.int32, (8, C), 1)
    blocks = []
    for i in range(C // SUB):
        r0 = SUB * i
        halves = [scs[i][0:8], scs[i][8:16]] if i in scs else [jnp.zeros((8, C), f32)] * 2
        for sl_ in range(SUB):
            s = r0 + sl_
            hs, sr = divmod(sl_, 8)
            cs = c_ref[s:s + 1, :]
            for h in (0, 1):
                if (h < hs and not rev) or (h > hs and rev):
                    continue
                h0 = r0 + 8 * h
                d = b2[h0:h0 + 8] - cs
                if h == hs:
                    d = jnp.where((row8 <= sr) if rev else (row8 >= sr), d, -jnp.inf)
                w = jnp.sum(q[h0:h0 + 8] * jnp.exp2(d), -1, keepdims=True)
                halves[h] = jnp.where(lane == s, w, halves[h])
            if sl_ % 8 == 7:
                yield
        blocks += halves
    o_intra = _dot(jnp.concatenate(blocks, 0).astype(bf16), vb)
    yield
    return o_inter + o_intra, St * jnp.exp(b_end) + dS


def _bd(x, lo):
    zero = jnp.zeros_like(x)
    return jnp.concatenate([jnp.where(lo, x, zero), jnp.where(lo, zero, x)], 0)


def _tri_inv(L, lo, eye, blk16, blk32):
    Lb = jnp.where(blk16, L, 0.0)
    T = eye + Lb
    Xb = Lb.astype(bf16)
    X2 = _dot(Xb, _bd(Xb, lo))
    yield
    for level in range(3):
        Xb = X2.astype(bf16)
        Xbd = _bd(Xb, lo)
        TX = _dot(T.astype(bf16), Xbd)
        if level < 2:
            X2 = _dot(Xb, Xbd)
        yield
        T = T + TX
    for off_mask in (blk32 & jnp.logical_not(blk16), jnp.logical_not(blk32)):
        off = jnp.where(off_mask, L, 0.0).astype(bf16)
        Tb = T.astype(bf16)
        A = _dot(Tb, _bd(off, lo))
        yield
        A = _dot(A.astype(bf16), _bd(Tb, lo))
        yield
        T = T + A
    return T


def _rwkv_chunk(r, v, kk, lw, k, bb, Gt, *, rev, emit, tri, masks):
    C = CHUNK
    lo, pmask, bdmask, eye, blk16, blk32 = masks
    g = _cumsum3(lw, tri)
    yield
    e = 0 if rev else C - 1
    g_end = jnp.exp(g[e:e + 1])
    ieg = jnp.exp(-g)
    at = -kk * jnp.exp(g - lw)
    bt, kt = bb * ieg, k * ieg
    if emit:
        AR = jnp.concatenate([at, r * jnp.exp(g)], 0).astype(bf16)
    else:
        AR = at.astype(bf16)
    P2 = _dot_nt(AR, Gt.astype(bf16))
    P1 = _dot_nt(AR, jnp.concatenate([_bd(bt, lo), _bd(kt, lo)], 0).astype(bf16))
    BKe = (jnp.concatenate([bt, kt], 0) * g_end).astype(bf16)
    yield
    vb = v.astype(bf16)
    vbd = _bd(vb, lo)
    P1 = jnp.where(pmask, P1, 0.0)
    lak = _dot(P1[0:C, 2 * C:4 * C].astype(bf16), vbd)
    if emit:
        Mr = P1[C:2 * C].astype(bf16)
    T = yield from _tri_inv(P1[0:C, 0:2 * C], lo, eye, blk16, blk32)
    Xp = (P2[0:C] + lak).astype(bf16)
    U = _dot(T.astype(bf16), _bd(Xp, lo))
    yield
    Ub = U.astype(bf16)
    dG = _dot_tn(jnp.concatenate([Ub, vb], 0), BKe)
    if emit:
        Y = _dot(Mr, jnp.concatenate([_bd(Ub, lo), vbd], 0))
    yield
    Gt_new = Gt * g_end + jnp.where(bdmask, dG, 0.0)
    if not emit:
        return None, Gt_new
    return P2[C:2 * C] + Y, Gt_new


def _scan_body(*refs, emit):
    fwd = refs[0:9]
    bwd = refs[9:18]
    hs0_ref, gs0_ref = refs[18:20]
    if emit:
        ho_f, ry_f, ho_b, ry_b, hsT_ref, gsT_ref, hs, gs, crow = refs[20:29]
    else:
        hsT_ref, gsT_ref, hs, gs = refs[20:24]
        crow = None
    j = pl.program_id(1)
    C = CHUNK

    @pl.when(j == 0)
    def _():
        hs[...] = hs0_ref[...]
        gs[...] = gs0_ref[...]

    ri = lax.broadcasted_iota(jnp.int32, (C, C), 0)
    ci = lax.broadcasted_iota(jnp.int32, (C, C), 1)
    tri_f = (ci <= ri).astype(bf16)
    tri_b = (ci >= ri).astype(bf16)
    rows = 2 * C if emit else C
    R = lax.broadcasted_iota(jnp.int32, (rows, 4 * C), 0)
    Cc = lax.broadcasted_iota(jnp.int32, (rows, 4 * C), 1)
    t, s = R & (C - 1), Cc & (C - 1)
    diag_ok = (R >= C) & (s == t)
    pmask_f = (s < t) | diag_ok
    pmask_b = (s > t) | diag_ok
    R2 = lax.broadcasted_iota(jnp.int32, (2 * C, 2 * C), 0)
    C2 = lax.broadcasted_iota(jnp.int32, (2 * C, 2 * C), 1)
    bdmask = (R2 < RW_HEAD) == (C2 < RW_HEAD)
    rp = lax.broadcasted_iota(jnp.int32, (C, 2 * C), 0)
    lp = lax.broadcasted_iota(jnp.int32, (C, 2 * C), 1)
    sp = lp & (C - 1)
    lo = lp < RW_HEAD
    eye = (rp == sp).astype(f32)
    blk16 = (rp // SUB) == (sp // SUB)
    blk32 = (rp // (2 * SUB)) == (sp // (2 * SUB))
    masks_f = (lo, pmask_f, bdmask, eye, blk16, blk32)
    masks_b = (lo, pmask_b, bdmask, eye, blk16, blk32)

    rw_chains, hg_chains, where = [], [], []
    for bi in range(hs.shape[1]):
        for p in range(NPAIR):
            for d, (src, rev, tri, masks) in enumerate(
                    ((fwd, False, tri_f, masks_f), (bwd, True, tri_b, masks_b))):
                hq, hv, hf, r, v, kk, lw, k, bb = [ref[bi, p] for ref in src]
                c_ref = crow.at[d, bi, p] if emit else None
                hg_chains.append(_hgrn_chunk(hq, hv, hf, hs[d, bi, p], c_ref, rev=rev, emit=emit, tri=tri))
                rw_chains.append(_rwkv_chunk(r, v, kk, lw, k, bb, gs[d, bi, p], rev=rev, emit=emit, tri=tri,
                                             masks=masks))
                where.append((d, bi, p, rev))
    def delayed(gen, rounds):
        for _ in range(rounds):
            yield
        return (yield from gen)

    per_elem = 2 * NPAIR
    order = rw_chains + hg_chains
    order = [delayed(g, SCAN_STAGGER * ((n % len(where)) // per_elem)) for n, g in enumerate(order)]
    driver = _lockstep(order)
    try:
        while True:
            next(driver)
    except StopIteration as stop:
        results = stop.value
    for n, (d, bi, p, rev) in enumerate(where):
        (y, Gt_new), (o, St_new) = results[n], results[len(where) + n]
        hs[d, bi, p] = St_new
        gs[d, bi, p] = Gt_new
        if emit:
            (ho_b if rev else ho_f)[bi, p] = o.astype(bf16)
            (ry_b if rev else ry_f)[bi, p] = y.astype(bf16)

    @pl.when(j == pl.num_programs(1) - 1)
    def _():
        hsT_ref[...] = hs[...]
        gsT_ref[...] = gs[...]


def _scan_call(prep, hs0, gs0, *, emit):
    B, _, T, _ = prep["hq"].shape
    n = T // CHUNK
    nb = SCAN_BATCH if B % SCAN_BATCH == 0 else 1
    f_spec = pl.BlockSpec((nb, NPAIR, CHUNK, LANES), lambda b, j: (b, 0, j, 0))
    b_spec = pl.BlockSpec((nb, NPAIR, CHUNK, LANES), lambda b, j: (b, 0, n - 1 - j, 0))
    st_spec = pl.BlockSpec((2, nb, NPAIR, LANES, LANES), lambda b, j: (0, b, 0, 0, 0))
    fwd = [prep[nm] for nm in ("hq", "hv", "hff", "r", "v", "kk", "lwf", "kf", "bf")]
    bwd = [prep[nm] for nm in ("hq", "hv", "hfb", "r", "v", "kk", "lwb", "kb", "bb")]
    seq_shape = jax.ShapeDtypeStruct((B, NPAIR, T, LANES), bf16)
    st_shape = jax.ShapeDtypeStruct((2, B, NPAIR, LANES, LANES), f32)
    out_specs = ([f_spec, f_spec, b_spec, b_spec] if emit else []) + [st_spec, st_spec]
    out_shape = ([seq_shape] * 4 if emit else []) + [st_shape, st_shape]
    return pl.pallas_call(
        functools.partial(_scan_body, emit=emit),
        grid=(B // nb, n),
        in_specs=[f_spec] * 9 + [b_spec] * 9 + [st_spec, st_spec],
        out_specs=out_specs,
        out_shape=out_shape,
        scratch_shapes=[pltpu.VMEM((2, nb, NPAIR, LANES, LANES), f32), pltpu.VMEM((2, nb, NPAIR, LANES, LANES), f32)]
        + ([pltpu.VMEM((2, nb, NPAIR, CHUNK, LANES), f32)] if emit else []),
        compiler_params=pltpu.CompilerParams(
            dimension_semantics=("parallel", "arbitrary"), vmem_limit_bytes=VMEM_LIMIT),
        name="scan_latent" if emit else "scan_ctx",
    )(*fwd, *bwd, hs0, gs0)


def _tail_body(hof, hob, hgate, ryf, ryb, g, bonus, x_ref, g1_ref, sh_ref, sc_ref, g2_ref, wout_ref, hgw_ref,
               lnx_ref, ln1_ref, wg_ref, wu_ref, wd_ref, ln2_ref, o_ref, *, alpha):
    tm = x_ref.shape[1] // TAIL_SPLIT
    lane_lo = lax.broadcasted_iota(jnp.int32, (tm, LANES), 1) < RW_HEAD

    def rows_chain(r0):
        rows = slice(r0, r0 + tm)
        y_hg, y_rw = [], []
        for p in range(NPAIR):
            o = hof[0, p, rows].astype(f32) + hob[0, p, rows].astype(f32)
            o = o * lax.rsqrt(jnp.mean(o * o, -1, keepdims=True) + HG_NORM_EPS) * hgw_ref[...]
            y_hg.append((o * hgate[0, p, rows].astype(f32)).astype(bf16))
            yr = ryf[0, p, rows].astype(f32) + ryb[0, p, rows].astype(f32)
            m = _seg64_sum(yr, lane_lo) * (1.0 / RW_HEAD)
            yc = yr - m
            var = _seg64_sum(yc * yc, lane_lo) * (1.0 / RW_HEAD)
            sl = slice(p * LANES, (p + 1) * LANES)
            yn = yc * lax.rsqrt(var + RW_GN_EPS) * lnx_ref[0:1, sl] + lnx_ref[1:2, sl]
            y_rw.append(((yn + bonus[0, p, rows].astype(f32)) * g[0, p, rows].astype(f32)).astype(bf16))
        mix = _dot(jnp.concatenate(y_hg + y_rw, -1), wout_ref[...])
        yield
        x1 = _ln(alpha * x_ref[0, rows] + g1_ref[0] * mix, LN_EPS) * ln1_ref[0:1] + ln1_ref[1:2]
        u = (_ln(x1, ADA_EPS) * (1.0 + sc_ref[0]) + sh_ref[0]).astype(bf16)
        hg, hu = _dot(u, wg_ref[...]), _dot(u, wu_ref[...])
        yield
        ffn = _dot((_silu(hg) * hu).astype(bf16), wd_ref[...])
        yield
        xn = _ln(alpha * x1 + g2_ref[0] * ffn, LN_EPS)
        o_ref[0, rows] = xn * ln2_ref[0:1] + ln2_ref[1:2]

    def delayed(gen, rounds):
        for _ in range(rounds):
            yield
        return (yield from gen)

    for _ in _lockstep([delayed(rows_chain(n * tm), n) for n in range(TAIL_SPLIT)]):
        pass


def _tail_call(scan_outs, prep, x, g1, sh2, sc2, g2, wout, hgw, lnx, ln1, wg, wu, wd, ln2, *, alpha):
    B, T, D = x.shape
    tm = TAIL_TM
    seq_spec = pl.BlockSpec((1, NPAIR, tm, LANES), lambda b, i: (b, 0, i, 0))
    x_spec = pl.BlockSpec((1, tm, D), lambda b, i: (b, i, 0))
    vec_spec = pl.BlockSpec((1, 1, D), lambda b, i: (b, 0, 0))
    ho_f, ry_f, ho_b, ry_b = scan_outs
    consts = (wout, hgw, lnx, ln1, wg, wu, wd, ln2)
    return pl.pallas_call(
        functools.partial(_tail_body, alpha=alpha),
        grid=(B, T // tm),
        in_specs=[seq_spec] * 7 + [x_spec] + [vec_spec] * 4 + [_const_spec(a.shape) for a in consts],
        out_specs=x_spec,
        out_shape=jax.ShapeDtypeStruct((B, T, D), f32),
        compiler_params=pltpu.CompilerParams(
            dimension_semantics=("parallel", "parallel"), vmem_limit_bytes=VMEM_LIMIT),
        name="mix_ffn",
    )(ho_f, ho_b, prep["hgate"], ry_f, ry_b, prep["g"], prep["bonus"], x, g1, sh2, sc2, g2, *consts)


def kernel(x, c, ctx, c_ctx, w_ada, b_ada, w_in, hgrn_lb_logits, hgrn_norm_w, rwkv_mu, rwkv_w0, rwkv_w2, rwkv_a0, rwkv_a2, rwkv_g2, rwkv_k_k, rwkv_k_a, rwkv_r_k, rwkv_lnx_w, rwkv_lnx_b, w_out, ln1_g, ln1_b, w_ffn_gate, w_ffn_up, w_ffn_down, ln2_g, ln2_b):
    B, T, D = x.shape
    depth = w_ada.shape[0]
    assert depth == 1 and T % 256 == 0 and T % TAIL_TM == 0 and ctx.shape[1] % CHUNK == 0
    alpha = (2.0 * depth) ** 0.25

    lb = jnp.cumsum(jax.nn.softmax(hgrn_lb_logits.astype(f32), axis=0), axis=0)[0]
    w_h = w_in[0, :, :HG_COLS].astype(bf16)
    w_r = jnp.pad(w_in[0, :, HG_COLS:], ((0, 0), (0, RW_COLS_PAD - RW_COLS))).astype(bf16)
    mu = jnp.pad(rwkv_mu[0], (0, RW_COLS_PAD - RW_COLS))[None]
    w_lora = jnp.zeros((LANES, 4 * RW_WIDTH), f32)
    for blk, w in enumerate((rwkv_w2[0, 0], rwkv_w2[0, 1], rwkv_a2[0, 0], rwkv_a2[0, 1])):
        w_lora = w_lora.at[32 * blk:32 * (blk + 1), RW_WIDTH * blk:RW_WIDTH * (blk + 1)].set(w)
    w_lora = w_lora.astype(bf16)
    w0a0 = jnp.concatenate([rwkv_w0[0, 0], rwkv_w0[0, 1], rwkv_a0[0, 0], rwkv_a0[0, 1]])[None]
    g2p = jnp.pad(rwkv_g2[0], ((0, LANES - rwkv_g2.shape[1]), (0, 0))).astype(bf16)
    rwp = jnp.zeros((8, RW_WIDTH), f32).at[0].set(rwkv_k_k[0]).at[1].set(rwkv_k_a[0]).at[2].set(
        rwkv_r_k[0].reshape(-1))
    wts = (w_h, w_r, mu, lb, w_lora, w0a0, g2p, rwp)

    cs = jnp.zeros((16, D), f32).at[:B].set(c).at[B].set(c_ctx)
    mod = _mod_call(cs, w_ada[0], b_ada[0][None])
    sh1, sc1, g1, sh2, sc2, g2 = [m[:B, None, :] for m in jnp.split(mod, 6, axis=-1)]
    ch1, cs1 = [jnp.broadcast_to(m[B:B + 1, None, :], (B, 1, D)) for m in jnp.split(mod, 6, axis=-1)[:2]]

    prep_ctx = dict(zip(_PREP_NAMES, _inproj_call(ctx, ch1, cs1, wts, grid_mode=False)))
    zeros_state = jnp.zeros((2, B, NPAIR, LANES, LANES), f32)
    hs_ctx, gs_ctx = _scan_call(prep_ctx, zeros_state, zeros_state, emit=False)

    prep = dict(zip(_PREP_NAMES, _inproj_call(x, sh1, sc1, wts, grid_mode=True)))
    ho_f, ry_f, ho_b, ry_b, _, _ = _scan_call(prep, hs_ctx, gs_ctx, emit=True)

    lnx = jnp.stack([rwkv_lnx_w[0], rwkv_lnx_b[0]])
    ln1 = jnp.stack([ln1_g[0], ln1_b[0]])
    ln2 = jnp.stack([ln2_g[0], ln2_b[0]])
    return _tail_call((ho_f, ry_f, ho_b, ry_b), prep, x, g1, sh2, sc2, g2, w_out[0].astype(bf16),
                      hgrn_norm_w[0][None], lnx, ln1, w_ffn_gate[0].astype(bf16), w_ffn_up[0].astype(bf16),
                      w_ffn_down[0].astype(bf16), ln2, alpha=alpha)
```

```python
import functools
import math

import jax
import jax.numpy as jnp
from jax import lax
from jax.experimental import pallas as pl
from jax.experimental.pallas import tpu as pltpu

f32 = jnp.float32
bf16 = jnp.bfloat16

LANES = 128
GRID_W = 64
HG_HEADS = 4
RW_HEADS = 8
RW_HEAD = 64
HG_WIDTH = 512
RW_WIDTH = 512
HG_COLS = 5 * HG_WIDTH
RW_COLS = 1760
RW_COLS_PAD = 1792
LORA_TILE = 12
GLORA_TILE = 13
NPAIR = 4
CHUNK = 64
SUB = 16
SCAN_BATCH = 4
INPROJ_TM = 512
TAIL_TM = 512
TAIL_SPLIT = 2
SCAN_STAGGER = 4
ADA_EPS = 1e-6
LN_EPS = 1e-5
HG_NORM_EPS = 1e-5
RW_GN_EPS = 64e-5
EXP_M05 = math.exp(-0.5)
LOG2E = math.log2(math.e)
VMEM_LIMIT = 60 * 1024 * 1024

_NT = (((1,), (1,)), ((), ()))
_TN = (((0,), (0,)), ((), ()))


def _dot(a, b):
    return jnp.dot(a, b, preferred_element_type=f32)


def _dot_nt(a, b):
    return lax.dot_general(a, b, _NT, preferred_element_type=f32)


def _dot_tn(a, b):
    return lax.dot_general(a, b, _TN, preferred_element_type=f32)


def _ln(xf, eps):
    m = jnp.mean(xf, -1, keepdims=True)
    xc = xf - m
    v = jnp.mean(xc * xc, -1, keepdims=True)
    return xc * lax.rsqrt(v + eps)


def _silu(x):
    return x * jax.nn.sigmoid(x)


def _split3(x):
    x1 = x.astype(bf16)
    r1 = x - x1.astype(f32)
    x2 = r1.astype(bf16)
    x3 = (r1 - x2.astype(f32)).astype(bf16)
    return x1, x2, x3


def _seg64_sum(x, lane_lo):
    s_all = jnp.sum(x, -1, keepdims=True)
    s_lo = jnp.sum(jnp.where(lane_lo, x, 0.0), -1, keepdims=True)
    return jnp.where(lane_lo, s_lo, s_all - s_lo)


def _const_spec(shape):
    nd = len(shape)
    return pl.BlockSpec(shape, lambda *_: (0,) * nd, pipeline_mode=pl.Buffered(1))


def _mod_body(cs_ref, w_ref, b_ref, o_ref):
    a = _silu(cs_ref[...])
    a1, a2, _ = _split3(a)
    w1, w2, _ = _split3(w_ref[...])
    o_ref[...] = _dot(a1, w1) + _dot(a1, w2) + _dot(a2, w1) + b_ref[...]


def _mod_call(cs, w_ada, b_ada):
    rows, d = cs.shape
    n = w_ada.shape[1]
    tn = 1024
    return pl.pallas_call(
        _mod_body,
        grid=(n // tn,),
        in_specs=[
            pl.BlockSpec((rows, d), lambda j: (0, 0)),
            pl.BlockSpec((d, tn), lambda j: (0, j)),
            pl.BlockSpec((1, tn), lambda j: (0, j)),
        ],
        out_specs=pl.BlockSpec((rows, tn), lambda j: (0, j)),
        out_shape=jax.ShapeDtypeStruct((rows, n), f32),
        compiler_params=pltpu.CompilerParams(dimension_semantics=("parallel",)),
        name="mod",
    )(cs, w_ada, b_ada)


_PREP_NAMES = ("hq", "hv", "hff", "hfb", "hgate", "r", "v", "kk", "lwf", "lwb", "kf", "kb", "bf", "bb", "g", "bonus")
_PREP_BF16 = ("hq", "hv", "hgate", "r", "v", "kk", "kf", "kb", "bf", "bb", "g", "bonus")
N_PREP = len(_PREP_NAMES)


def _inproj_body(*refs, tm, pad, grid_mode):
    if grid_mode:
        x_ref, xu_ref, xd_ref = refs[:3]
        rest = refs[3:]
    else:
        x_ref = refs[0]
        rest = refs[1:]
    (sh_ref, sc_ref, wh_ref, wr_ref, mu_ref, lb_ref, wl_ref, w0a0_ref, g2_ref, rwp_ref) = rest[:10]
    outs = rest[10:10 + N_PREP]
    ext_ref = rest[10 + N_PREP]
    (hq_o, hv_o, hff_o, hfb_o, hgate_o, r_o, v_o, kk_o, lwf_o, lwb_o, kf_o, kb_o, bf_o, bb_o, g_o, bonus_o) = outs

    sh = sh_ref[0]
    sc1 = 1.0 + sc_ref[0]

    def modulate(xv):
        return (_ln(xv, ADA_EPS) * sc1 + sh).astype(bf16)

    u_main = modulate(x_ref[0])
    wr = wr_ref[...]
    if grid_mode:
        i = pl.program_id(1)
        nt = pl.num_programs(1)
        zr_ext = _dot(jnp.concatenate([modulate(xu_ref[0]), u_main, modulate(xd_ref[0])], 0), wr)
        zr_main = zr_ext[pad:pad + tm]
        ext_ref[0:pad] = jnp.where(i > 0, zr_ext[0:pad], 0.0)
        ext_ref[pad + tm:pad + tm + pad] = jnp.where(i < nt - 1, zr_ext[pad + tm:pad + tm + pad], 0.0)
    else:
        zr_main = _dot(u_main, wr)
        ext_ref[0:pad] = jnp.zeros((pad, RW_COLS_PAD), f32)
        ext_ref[pad + tm:pad + tm + pad] = jnp.zeros((pad, RW_COLS_PAD), f32)
    ext_ref[pad:pad + tm] = zr_main
    hg_split = 2 * HG_WIDTH
    zh_a = _dot(u_main, wh_ref[:, 0:hg_split])

    lane = lax.broadcasted_iota(jnp.int32, (tm, LANES), 1)
    if grid_mode:
        col = lax.broadcasted_iota(jnp.int32, (tm, LANES), 0) & (GRID_W - 1)
        not_first_col = col != 0
        not_last_col = col != GRID_W - 1
        bounds = [0, RW_COLS // 4, RW_COLS // 2, 3 * RW_COLS // 4, RW_COLS]
        windows = [(pad - 1, not_first_col), (pad + 1, not_last_col), (pad - GRID_W, None), (pad + GRID_W, None)]
    else:
        bounds = [0, RW_COLS // 2, RW_COLS]
        windows = [(pad - 1, None), (pad + 1, None)]

    zt = []
    for c in range(RW_COLS_PAD // LANES):
        lo, hi = c * LANES, (c + 1) * LANES
        cs = slice(lo, hi)
        z = zr_main[:, cs]
        shifted = jnp.zeros((tm, LANES), f32)
        for q, (off, valid) in enumerate(windows):
            qlo, qhi = bounds[q], bounds[q + 1]
            if max(lo, qlo) >= min(hi, qhi):
                continue
            win = ext_ref[off:off + tm, cs]
            m = valid
            if not (qlo <= lo and hi <= qhi):
                inq = (lane >= qlo - lo) & (lane < qhi - lo)
                m = inq if m is None else (m & inq)
            shifted = win if m is None else jnp.where(m, win, shifted)
        zt.append(z + mu_ref[:, cs] * (shifted - z))

    lane_lo = lane < RW_HEAD
    lora_in = jnp.where(lane_lo, jnp.tanh(zt[LORA_TILE]), zt[LORA_TILE]).astype(bf16)
    pre = _dot(lora_in, wl_ref[...]) + w0a0_ref[...]
    g_all = _dot(jax.nn.sigmoid(zt[GLORA_TILE]).astype(bf16), g2_ref[...])
    zh = jnp.concatenate([zh_a, _dot(u_main, wh_ref[:, hg_split:HG_COLS])], -1)

    for p in range(NPAIR):
        sl = slice(p * LANES, (p + 1) * LANES)
        r = zt[p]
        k = zt[NPAIR + p]
        v = zt[2 * NPAIR + p]
        lw_f = -jax.nn.sigmoid(pre[:, sl]) * EXP_M05
        lw_b = -jax.nn.sigmoid(pre[:, RW_WIDTH + p * LANES:RW_WIDTH + (p + 1) * LANES]) * EXP_M05
        a_f = jax.nn.sigmoid(pre[:, 2 * RW_WIDTH + p * LANES:2 * RW_WIDTH + (p + 1) * LANES])
        a_b = jax.nn.sigmoid(pre[:, 3 * RW_WIDTH + p * LANES:3 * RW_WIDTH + (p + 1) * LANES])
        kk = k * rwp_ref[0:1, sl]
        kk = kk / jnp.maximum(jnp.sqrt(_seg64_sum(kk * kk, lane_lo)), 1e-12)
        ka = rwp_ref[1:2, sl]
        k_f = k * (1.0 + (a_f - 1.0) * ka)
        k_b = k * (1.0 + (a_b - 1.0) * ka)
        bonus = _seg64_sum(r * rwp_ref[2:3, sl] * (k_f + k_b), lane_lo) * v
        r_o[0, p] = r.astype(bf16)
        v_o[0, p] = v.astype(bf16)
        kk_o[0, p] = kk.astype(bf16)
        lwf_o[0, p] = lw_f
        lwb_o[0, p] = lw_b
        kf_o[0, p] = k_f.astype(bf16)
        kb_o[0, p] = k_b.astype(bf16)
        bf_o[0, p] = (kk * a_f).astype(bf16)
        bb_o[0, p] = (kk * a_b).astype(bf16)
        g_o[0, p] = g_all[:, sl].astype(bf16)
        bonus_o[0, p] = bonus.astype(bf16)

    for p in range(NPAIR):
        sl = slice(p * LANES, (p + 1) * LANES)
        lb_f = lb_ref[0:1, sl]
        lb_b = lb_ref[1:2, sl]
        hq_o[0, p] = _silu(zh[:, p * LANES:(p + 1) * LANES]).astype(bf16)
        hff_o[0, p] = lb_f + (1.0 - lb_f) * jax.nn.sigmoid(zh[:, HG_WIDTH + p * LANES:HG_WIDTH + (p + 1) * LANES])
        hfb_o[0, p] = lb_b + (1.0 - lb_b) * jax.nn.sigmoid(
            zh[:, 2 * HG_WIDTH + p * LANES:2 * HG_WIDTH + (p + 1) * LANES])
        hv_o[0, p] = zh[:, 3 * HG_WIDTH + p * LANES:3 * HG_WIDTH + (p + 1) * LANES].astype(bf16)
        hgate_o[0, p] = _silu(zh[:, 4 * HG_WIDTH + p * LANES:4 * HG_WIDTH + (p + 1) * LANES]).astype(bf16)


def _inproj_call(x, shift, scale, wts, *, grid_mode):
    B, T, D = x.shape
    if grid_mode:
        tm, pad = INPROJ_TM, GRID_W
        rows_per_tile = tm // GRID_W
        n_rows = T // GRID_W
        x_specs = [
            pl.BlockSpec((1, tm, D), lambda b, i: (b, i, 0)),
            pl.BlockSpec((1, GRID_W, D), lambda b, i: (b, jnp.maximum(i * rows_per_tile - 1, 0), 0)),
            pl.BlockSpec((1, GRID_W, D), lambda b, i: (b, jnp.minimum((i + 1) * rows_per_tile, n_rows - 1), 0)),
        ]
        x_args = [x, x, x]
    else:
        tm, pad = T, 8
        x_specs = [pl.BlockSpec((1, tm, D), lambda b, i: (b, i, 0))]
        x_args = [x]
    vec_spec = pl.BlockSpec((1, 1, D), lambda b, i: (b, 0, 0))
    w_specs = [_const_spec(w.shape) for w in wts]
    out_spec = pl.BlockSpec((1, NPAIR, tm, LANES), lambda b, i: (b, 0, i, 0))
    out_shape = [jax.ShapeDtypeStruct((B, NPAIR, T, LANES), bf16 if nm in _PREP_BF16 else f32) for nm in _PREP_NAMES]
    return pl.pallas_call(
        functools.partial(_inproj_body, tm=tm, pad=pad, grid_mode=grid_mode),
        grid=(B, T // tm),
        in_specs=x_specs + [vec_spec, vec_spec] + w_specs,
        out_specs=[out_spec] * N_PREP,
        out_shape=out_shape,
        scratch_shapes=[pltpu.VMEM((tm + 2 * pad, RW_COLS_PAD), f32)],
        compiler_params=pltpu.CompilerParams(
            dimension_semantics=("parallel", "parallel"), vmem_limit_bytes=VMEM_LIMIT),
        name="inproj_grid" if grid_mode else "inproj_seq",
    )(*x_args, shift, scale, *wts)


def _cumsum3(x, tri):
    x1, x2, _ = _split3(x)
    return _dot(tri, x1) + _dot(tri, x2)


def _lockstep(gens):
    results = [None] * len(gens)
    active = list(enumerate(gens))
    while active:
        still = []
        for i, g in active:
            try:
                next(g)
                still.append((i, g))
            except StopIteration as stop:
                results[i] = stop.value
        active = still
        if active:
            yield
    return results


def _hgrn_chunk(q, v, f, St, c_ref, *, rev, emit, tri):
    C = CHUNK
    q = q.astype(f32)
    k = 1.0 - f
    b = _cumsum3(jnp.log(f), tri)
    yield
    e = 0 if rev else C - 1
    b_end = b[e:e + 1]
    kd = (k * jnp.exp(b_end - b)).astype(bf16)
    vb = v.astype(bf16)
    dS = _dot_tn(vb, kd)
    if not emit:
        yield
        return None, St * jnp.exp(b_end) + dS

    o_inter = _dot_nt((q * jnp.exp(b)).astype(bf16), St.astype(bf16))
    scs = {}
    for i in range(C // SUB):
        r0 = SUB * i
        if not rev and i > 0:
            ref, lo, hi = b[r0 - 1:r0], 0, r0
        elif rev and i < C // SUB - 1:
            ref, lo, hi = b[r0 + SUB:r0 + SUB + 1], r0 + SUB, C
        else:
            continue
        qi = (q[r0:r0 + SUB] * jnp.exp(b[r0:r0 + SUB] - ref)).astype(bf16)
        kj = (k[lo:hi] * jnp.exp(ref - b[lo:hi])).astype(bf16)
        pads = [jnp.zeros((n, LANES), bf16) for n in (lo, C - hi)]
        kj = jnp.concatenate([a for a in (pads[0], kj, pads[1]) if a.shape[0]], 0)
        scs[i] = _dot_nt(qi, kj)
    b2 = b * LOG2E
    c_ref[...] = b2 - jnp.log2(k)
    yield
    row8 = lax.broadcasted_iota(jnp.int32, (8, LANES), 0)
    lane = lax.broadcasted_iota(jnp.int32, (8, C), 1)
    blocks = []
    for i in range(C // SUB):
        r0 = SUB * i
        halves = [scs[i][0:8], scs[i][8:16]] if i in scs else [jnp.zeros((8, C), f32)] * 2
        for sl_ in range(SUB):
            s = r0 + sl_
            hs, sr = divmod(sl_, 8)
            cs = c_ref[s:s + 1, :]
            for h in (0, 1):
                if (h < hs and not rev) or (h > hs and rev):
                    continue
                h0 = r0 + 8 * h
                d = b2[h0:h0 + 8] - cs
                if h == hs:
                    d = jnp.where((row8 <= sr) if rev else (row8 >= sr), d, -jnp.inf)
                w = jnp.sum(q[h0:h0 + 8] * jnp.exp2(d), -1, keepdims=True)
                halves[h] = jnp.where(lane == s, w, halves[h])
            if sl_ % 8 == 7:
                yield
        blocks += halves
    o_intra = _dot(jnp.concatenate(blocks, 0).astype(bf16), vb)
    yield
    return o_inter + o_intra, St * jnp.exp(b_end) + dS


def _bd(x, lo):
    zero = jnp.zeros_like(x)
    return jnp.concatenate([jnp.where(lo, x, zero), jnp.where(lo, zero, x)], 0)


def _tri_inv(L, lo, eye, blk16, blk32):
    Lb = jnp.where(blk16, L, 0.0)
    T = eye + Lb
    Xb = Lb.astype(bf16)
    X2 = _dot(Xb, _bd(Xb, lo))
    yield
    for level in range(3):
        Xb = X2.astype(bf16)
        Xbd = _bd(Xb, lo)
        TX = _dot(T.astype(bf16), Xbd)
        if level < 2:
            X2 = _dot(Xb, Xbd)
        yield
        T = T + TX
    for off_mask in (blk32 & jnp.logical_not(blk16), jnp.logical_not(blk32)):
        off = jnp.where(off_mask, L, 0.0).astype(bf16)
        Tb = T.astype(bf16)
        A = _dot(Tb, _bd(off, lo))
        yield
        A = _dot(A.astype(bf16), _bd(Tb, lo))
        yield
        T = T + A
    return T


def _rwkv_chunk(r, v, kk, lw, k, bb, Gt, *, rev, emit, tri, masks):
    C = CHUNK
    lo, pmask, bdmask, eye, blk16, blk32 = masks
    r, kk, k, bb = [a.astype(f32) for a in (r, kk, k, bb)]
    g = _cumsum3(lw, tri)
    yield
    e = 0 if rev else C - 1
    g_end = jnp.exp(g[e:e + 1])
    ieg = jnp.exp(-g)
    at = -kk * jnp.exp(g - lw)
    bt, kt = bb * ieg, k * ieg
    if emit:
        AR = jnp.concatenate([at, r * jnp.exp(g)], 0).astype(bf16)
    else:
        AR = at.astype(bf16)
    P2 = _dot_nt(AR, Gt.astype(bf16))
    P1 = _dot_nt(AR, jnp.concatenate([_bd(bt.astype(bf16), lo), _bd(kt.astype(bf16), lo)], 0))
    BKe = (jnp.concatenate([bt, kt], 0) * g_end).astype(bf16)
    yield
    vb = v.astype(bf16)
    vbd = _bd(vb, lo)
    P1 = jnp.where(pmask, P1, 0.0)
    lak = _dot(P1[0:C, 2 * C:4 * C].astype(bf16), vbd)
    if emit:
        Mr = P1[C:2 * C].astype(bf16)
    T = yield from _tri_inv(P1[0:C, 0:2 * C], lo, eye, blk16, blk32)
    Xp = (P2[0:C] + lak).astype(bf16)
    U = _dot(T.astype(bf16), _bd(Xp, lo))
    yield
    Ub = U.astype(bf16)
    dG = _dot_tn(jnp.concatenate([Ub, vb], 0), BKe)
    if emit:
        Y = _dot(Mr, jnp.concatenate([_bd(Ub, lo), vbd], 0))
    yield
    Gt_new = Gt * g_end + jnp.where(bdmask, dG, 0.0)
    if not emit:
        return None, Gt_new
    return P2[C:2 * C] + Y, Gt_new


def _scan_body(*refs, emit):
    fwd = refs[0:9]
    bwd = refs[9:18]
    hs0_ref, gs0_ref = refs[18:20]
    if emit:
        ho_f, ry_f, ho_b, ry_b, hsT_ref, gsT_ref, hs, gs, crow = refs[20:29]
    else:
        hsT_ref, gsT_ref, hs, gs = refs[20:24]
        crow = None
    j = pl.program_id(1)
    C = CHUNK

    @pl.when(j == 0)
    def _():
        hs[...] = hs0_ref[...]
        gs[...] = gs0_ref[...]

    ri = lax.broadcasted_iota(jnp.int32, (C, C), 0)
    ci = lax.broadcasted_iota(jnp.int32, (C, C), 1)
    tri_f = (ci <= ri).astype(bf16)
    tri_b = (ci >= ri).astype(bf16)
    rows = 2 * C if emit else C
    R = lax.broadcasted_iota(jnp.int32, (rows, 4 * C), 0)
    Cc = lax.broadcasted_iota(jnp.int32, (rows, 4 * C), 1)
    t, s = R & (C - 1), Cc & (C - 1)
    diag_ok = (R >= C) & (s == t)
    pmask_f = (s < t) | diag_ok
    pmask_b = (s > t) | diag_ok
    R2 = lax.broadcasted_iota(jnp.int32, (2 * C, 2 * C), 0)
    C2 = lax.broadcasted_iota(jnp.int32, (2 * C, 2 * C), 1)
    bdmask = (R2 < RW_HEAD) == (C2 < RW_HEAD)
    rp = lax.broadcasted_iota(jnp.int32, (C, 2 * C), 0)
    lp = lax.broadcasted_iota(jnp.int32, (C, 2 * C), 1)
    sp = lp & (C - 1)
    lo = lp < RW_HEAD
    eye = (rp == sp).astype(f32)
    blk16 = (rp // SUB) == (sp // SUB)
    blk32 = (rp // (2 * SUB)) == (sp // (2 * SUB))
    masks_f = (lo, pmask_f, bdmask, eye, blk16, blk32)
    masks_b = (lo, pmask_b, bdmask, eye, blk16, blk32)

    rw_chains, hg_chains, where = [], [], []
    for bi in range(hs.shape[1]):
        for p in range(NPAIR):
            for d, (src, rev, tri, masks) in enumerate(
                    ((fwd, False, tri_f, masks_f), (bwd, True, tri_b, masks_b))):
                hq, hv, hf, r, v, kk, lw, k, bb = [ref[bi, p] for ref in src]
                c_ref = crow.at[d, bi, p] if emit else None
                hg_chains.append(_hgrn_chunk(hq, hv, hf, hs[d, bi, p], c_ref, rev=rev, emit=emit, tri=tri))
                rw_chains.append(_rwkv_chunk(r, v, kk, lw, k, bb, gs[d, bi, p], rev=rev, emit=emit, tri=tri,
                                             masks=masks))
                where.append((d, bi, p, rev))
    def delayed(gen, rounds):
        for _ in range(rounds):
            yield
        return (yield from gen)

    per_elem = 2 * NPAIR
    order = rw_chains + hg_chains
    order = [delayed(g, SCAN_STAGGER * ((n % len(where)) // per_elem)) for n, g in enumerate(order)]
    driver = _lockstep(order)
    try:
        while True:
            next(driver)
    except StopIteration as stop:
        results = stop.value
    for n, (d, bi, p, rev) in enumerate(where):
        (y, Gt_new), (o, St_new) = results[n], results[len(where) + n]
        hs[d, bi, p] = St_new
        gs[d, bi, p] = Gt_new
        if emit:
            (ho_b if rev else ho_f)[bi, p] = o.astype(bf16)
            (ry_b if rev else ry_f)[bi, p] = y.astype(bf16)

    @pl.when(j == pl.num_programs(1) - 1)
    def _():
        hsT_ref[...] = hs[...]
        gsT_ref[...] = gs[...]


def _scan_call(prep, hs0, gs0, *, emit):
    B, _, T, _ = prep["hq"].shape
    n = T // CHUNK
    nb = SCAN_BATCH if B % SCAN_BATCH == 0 else 1
    f_spec = pl.BlockSpec((nb, NPAIR, CHUNK, LANES), lambda b, j: (b, 0, j, 0))
    b_spec = pl.BlockSpec((nb, NPAIR, CHUNK, LANES), lambda b, j: (b, 0, n - 1 - j, 0))
    st_spec = pl.BlockSpec((2, nb, NPAIR, LANES, LANES), lambda b, j: (0, b, 0, 0, 0))
    fwd = [prep[nm] for nm in ("hq", "hv", "hff", "r", "v", "kk", "lwf", "kf", "bf")]
    bwd = [prep[nm] for nm in ("hq", "hv", "hfb", "r", "v", "kk", "lwb", "kb", "bb")]
    seq_shape = jax.ShapeDtypeStruct((B, NPAIR, T, LANES), bf16)
    st_shape = jax.ShapeDtypeStruct((2, B, NPAIR, LANES, LANES), f32)
    out_specs = ([f_spec, f_spec, b_spec, b_spec] if emit else []) + [st_spec, st_spec]
    out_shape = ([seq_shape] * 4 if emit else []) + [st_shape, st_shape]
    return pl.pallas_call(
        functools.partial(_scan_body, emit=emit),
        grid=(B // nb, n),
        in_specs=[f_spec] * 9 + [b_spec] * 9 + [st_spec, st_spec],
        out_specs=out_specs,
        out_shape=out_shape,
        scratch_shapes=[pltpu.VMEM((2, nb, NPAIR, LANES, LANES), f32), pltpu.VMEM((2, nb, NPAIR, LANES, LANES), f32)]
        + ([pltpu.VMEM((2, nb, NPAIR, CHUNK, LANES), f32)] if emit else []),
        compiler_params=pltpu.CompilerParams(
            dimension_semantics=("parallel", "arbitrary"), vmem_limit_bytes=VMEM_LIMIT),
        name="scan_latent" if emit else "scan_ctx",
    )(*fwd, *bwd, hs0, gs0)


def _tail_body(hof, hob, hgate, ryf, ryb, g, bonus, x_ref, g1_ref, sh_ref, sc_ref, g2_ref, wout_ref, hgw_ref,
               lnx_ref, ln1_ref, wg_ref, wu_ref, wd_ref, ln2_ref, o_ref, *, alpha):
    tm = x_ref.shape[1] // TAIL_SPLIT
    lane_lo = lax.broadcasted_iota(jnp.int32, (tm, LANES), 1) < RW_HEAD

    def rows_chain(r0):
        rows = slice(r0, r0 + tm)
        y_hg, y_rw = [], []
        for p in range(NPAIR):
            o = hof[0, p, rows].astype(f32) + hob[0, p, rows].astype(f32)
            o = o * lax.rsqrt(jnp.mean(o * o, -1, keepdims=True) + HG_NORM_EPS) * hgw_ref[...]
            y_hg.append((o * hgate[0, p, rows].astype(f32)).astype(bf16))
            yr = ryf[0, p, rows].astype(f32) + ryb[0, p, rows].astype(f32)
            m = _seg64_sum(yr, lane_lo) * (1.0 / RW_HEAD)
            yc = yr - m
            var = _seg64_sum(yc * yc, lane_lo) * (1.0 / RW_HEAD)
            sl = slice(p * LANES, (p + 1) * LANES)
            yn = yc * lax.rsqrt(var + RW_GN_EPS) * lnx_ref[0:1, sl] + lnx_ref[1:2, sl]
            y_rw.append(((yn + bonus[0, p, rows].astype(f32)) * g[0, p, rows].astype(f32)).astype(bf16))
        mix = _dot(jnp.concatenate(y_hg + y_rw, -1), wout_ref[...])
        yield
        x1 = _ln(alpha * x_ref[0, rows] + g1_ref[0] * mix, LN_EPS) * ln1_ref[0:1] + ln1_ref[1:2]
        u = (_ln(x1, ADA_EPS) * (1.0 + sc_ref[0]) + sh_ref[0]).astype(bf16)
        hg, hu = _dot(u, wg_ref[...]), _dot(u, wu_ref[...])
        yield
        ffn = _dot((_silu(hg) * hu).astype(bf16), wd_ref[...])
        yield
        xn = _ln(alpha * x1 + g2_ref[0] * ffn, LN_EPS)
        o_ref[0, rows] = xn * ln2_ref[0:1] + ln2_ref[1:2]

    def delayed(gen, rounds):
        for _ in range(rounds):
            yield
        return (yield from gen)

    for _ in _lockstep([delayed(rows_chain(n * tm), n) for n in range(TAIL_SPLIT)]):
        pass


def _tail_call(scan_outs, prep, x, g1, sh2, sc2, g2, wout, hgw, lnx, ln1, wg, wu, wd, ln2, *, alpha):
    B, T, D = x.shape
    tm = TAIL_TM
    seq_spec = pl.BlockSpec((1, NPAIR, tm, LANES), lambda b, i: (b, 0, i, 0))
    x_spec = pl.BlockSpec((1, tm, D), lambda b, i: (b, i, 0))
    vec_spec = pl.BlockSpec((1, 1, D), lambda b, i: (b, 0, 0))
    ho_f, ry_f, ho_b, ry_b = scan_outs
    consts = (wout, hgw, lnx, ln1, wg, wu, wd, ln2)
    return pl.pallas_call(
        functools.partial(_tail_body, alpha=alpha),
        grid=(B, T // tm),
        in_specs=[seq_spec] * 7 + [x_spec] + [vec_spec] * 4 + [_const_spec(a.shape) for a in consts],
        out_specs=x_spec,
        out_shape=jax.ShapeDtypeStruct((B, T, D), f32),
        compiler_params=pltpu.CompilerParams(
            dimension_semantics=("parallel", "parallel"), vmem_limit_bytes=VMEM_LIMIT),
        name="mix_ffn",
    )(ho_f, ho_b, prep["hgate"], ry_f, ry_b, prep["g"], prep["bonus"], x, g1, sh2, sc2, g2, *consts)


def kernel(x, c, ctx, c_ctx, w_ada, b_ada, w_in, hgrn_lb_logits, hgrn_norm_w, rwkv_mu, rwkv_w0, rwkv_w2, rwkv_a0, rwkv_a2, rwkv_g2, rwkv_k_k, rwkv_k_a, rwkv_r_k, rwkv_lnx_w, rwkv_lnx_b, w_out, ln1_g, ln1_b, w_ffn_gate, w_ffn_up, w_ffn_down, ln2_g, ln2_b):
    B, T, D = x.shape
    depth = w_ada.shape[0]
    assert depth == 1 and T % INPROJ_TM == 0 and T % TAIL_TM == 0 and ctx.shape[1] % CHUNK == 0
    alpha = (2.0 * depth) ** 0.25

    lb = jnp.cumsum(jax.nn.softmax(hgrn_lb_logits.astype(f32), axis=0), axis=0)[0]
    w_h = w_in[0, :, :HG_COLS].astype(bf16)
    w_r = jnp.pad(w_in[0, :, HG_COLS:], ((0, 0), (0, RW_COLS_PAD - RW_COLS))).astype(bf16)
    mu = jnp.pad(rwkv_mu[0], (0, RW_COLS_PAD - RW_COLS))[None]
    w_lora = jnp.zeros((LANES, 4 * RW_WIDTH), f32)
    for blk, w in enumerate((rwkv_w2[0, 0], rwkv_w2[0, 1], rwkv_a2[0, 0], rwkv_a2[0, 1])):
        w_lora = w_lora.at[32 * blk:32 * (blk + 1), RW_WIDTH * blk:RW_WIDTH * (blk + 1)].set(w)
    w_lora = w_lora.astype(bf16)
    w0a0 = jnp.concatenate([rwkv_w0[0, 0], rwkv_w0[0, 1], rwkv_a0[0, 0], rwkv_a0[0, 1]])[None]
    g2p = jnp.pad(rwkv_g2[0], ((0, LANES - rwkv_g2.shape[1]), (0, 0))).astype(bf16)
    rwp = jnp.zeros((8, RW_WIDTH), f32).at[0].set(rwkv_k_k[0]).at[1].set(rwkv_k_a[0]).at[2].set(
        rwkv_r_k[0].reshape(-1))
    wts = (w_h, w_r, mu, lb, w_lora, w0a0, g2p, rwp)

    cs = jnp.zeros((16, D), f32).at[:B].set(c).at[B].set(c_ctx)
    mod = _mod_call(cs, w_ada[0], b_ada[0][None])
    sh1, sc1, g1, sh2, sc2, g2 = [m[:B, None, :] for m in jnp.split(mod, 6, axis=-1)]
    ch1, cs1 = [jnp.broadcast_to(m[B:B + 1, None, :], (B, 1, D)) for m in jnp.split(mod, 6, axis=-1)[:2]]

    prep_ctx = dict(zip(_PREP_NAMES, _inproj_call(ctx, ch1, cs1, wts, grid_mode=False)))
    zeros_state = jnp.zeros((2, B, NPAIR, LANES, LANES), f32)
    hs_ctx, gs_ctx = _scan_call(prep_ctx, zeros_state, zeros_state, emit=False)

    prep = dict(zip(_PREP_NAMES, _inproj_call(x, sh1, sc1, wts, grid_mode=True)))
    ho_f, ry_f, ho_b, ry_b, _, _ = _scan_call(prep, hs_ctx, gs_ctx, emit=True)

    lnx = jnp.stack([rwkv_lnx_w[0], rwkv_lnx_b[0]])
    ln1 = jnp.stack([ln1_g[0], ln1_b[0]])
    ln2 = jnp.stack([ln2_g[0], ln2_b[0]])
    return _tail_call((ho_f, ry_f, ho_b, ry_b), prep, x, g1, sh2, sc2, g2, w_out[0].astype(bf16),
                      hgrn_norm_w[0][None], lnx, ln1, w_ffn_gate[0].astype(bf16), w_ffn_up[0].astype(bf16),
                      w_ffn_down[0].astype(bf16), ln2, alpha=alpha)
```

```python
import functools
import math

import jax
import jax.numpy as jnp
from jax import lax
from jax.experimental import pallas as pl
from jax.experimental.pallas import tpu as pltpu

f32 = jnp.float32
bf16 = jnp.bfloat16

LANES = 128
GRID_W = 64
HG_HEADS = 4
RW_HEADS = 8
RW_HEAD = 64
HG_WIDTH = 512
RW_WIDTH = 512
HG_COLS = 5 * HG_WIDTH
RW_COLS = 1760
RW_COLS_PAD = 1792
LORA_TILE = 12
GLORA_TILE = 13
NPAIR = 4
CHUNK = 64
SUB = 16
SCAN_BATCH = 4
INPROJ_TM = 512
TAIL_TM = 512
TAIL_SPLIT = 2
SCAN_STAGGER = 4
ADA_EPS = 1e-6
LN_EPS = 1e-5
HG_NORM_EPS = 1e-5
RW_GN_EPS = 64e-5
EXP_M05 = math.exp(-0.5)
LOG2E = math.log2(math.e)
VMEM_LIMIT = 60 * 1024 * 1024

_NT = (((1,), (1,)), ((), ()))
_TN = (((0,), (0,)), ((), ()))


def _dot(a, b):
    return jnp.dot(a, b, preferred_element_type=f32)


def _dot_nt(a, b):
    return lax.dot_general(a, b, _NT, preferred_element_type=f32)


def _dot_tn(a, b):
    return lax.dot_general(a, b, _TN, preferred_element_type=f32)


def _ln(xf, eps):
    m = jnp.mean(xf, -1, keepdims=True)
    xc = xf - m
    v = jnp.mean(xc * xc, -1, keepdims=True)
    return xc * lax.rsqrt(v + eps)


def _sigmoid(x):
    return 0.5 * jnp.tanh(0.5 * x) + 0.5


def _silu(x):
    return x * _sigmoid(x)


def _split3(x):
    x1 = x.astype(bf16)
    r1 = x - x1.astype(f32)
    x2 = r1.astype(bf16)
    x3 = (r1 - x2.astype(f32)).astype(bf16)
    return x1, x2, x3


def _seg64_sum(x, lane_lo):
    s_all = jnp.sum(x, -1, keepdims=True)
    s_lo = jnp.sum(jnp.where(lane_lo, x, 0.0), -1, keepdims=True)
    return jnp.where(lane_lo, s_lo, s_all - s_lo)


def _const_spec(shape):
    nd = len(shape)
    return pl.BlockSpec(shape, lambda *_: (0,) * nd, pipeline_mode=pl.Buffered(1))


def _mod_body(cs_ref, w_ref, b_ref, o_ref):
    a = _silu(cs_ref[...])
    a1, a2, _ = _split3(a)
    w1, w2, _ = _split3(w_ref[...])
    o_ref[...] = _dot(a1, w1) + _dot(a1, w2) + _dot(a2, w1) + b_ref[...]


def _mod_call(cs, w_ada, b_ada):
    rows, d = cs.shape
    n = w_ada.shape[1]
    tn = 1024
    return pl.pallas_call(
        _mod_body,
        grid=(n // tn,),
        in_specs=[
            pl.BlockSpec((rows, d), lambda j: (0, 0)),
            pl.BlockSpec((d, tn), lambda j: (0, j)),
            pl.BlockSpec((1, tn), lambda j: (0, j)),
        ],
        out_specs=pl.BlockSpec((rows, tn), lambda j: (0, j)),
        out_shape=jax.ShapeDtypeStruct((rows, n), f32),
        compiler_params=pltpu.CompilerParams(dimension_semantics=("parallel",)),
        name="mod",
    )(cs, w_ada, b_ada)


_PREP_NAMES = ("hq", "hv", "hff", "hfb", "hgate", "r", "v", "kk", "lwf", "lwb", "kf", "kb", "bf", "bb", "g", "bonus")
_PREP_BF16 = ("hq", "hv", "hgate", "r", "v", "kk", "kf", "kb", "bf", "bb", "g", "bonus")
N_PREP = len(_PREP_NAMES)


def _inproj_body(*refs, tm, pad, grid_mode):
    if grid_mode:
        x_ref, xu_ref, xd_ref = refs[:3]
        rest = refs[3:]
    else:
        x_ref = refs[0]
        rest = refs[1:]
    (sh_ref, sc_ref, wh_ref, wr_ref, mu_ref, lb_ref, wl_ref, w0a0_ref, g2_ref, rwp_ref) = rest[:10]
    outs = rest[10:10 + N_PREP]
    ext_ref = rest[10 + N_PREP]
    (hq_o, hv_o, hff_o, hfb_o, hgate_o, r_o, v_o, kk_o, lwf_o, lwb_o, kf_o, kb_o, bf_o, bb_o, g_o, bonus_o) = outs

    sh = sh_ref[0]
    sc1 = 1.0 + sc_ref[0]

    def modulate(xv):
        return (_ln(xv, ADA_EPS) * sc1 + sh).astype(bf16)

    u_main = modulate(x_ref[0])
    wr = wr_ref[...]
    if grid_mode:
        i = pl.program_id(1)
        nt = pl.num_programs(1)
        zr_ext = _dot(jnp.concatenate([modulate(xu_ref[0]), u_main, modulate(xd_ref[0])], 0), wr)
        zr_main = zr_ext[pad:pad + tm]
        ext_ref[0:pad] = jnp.where(i > 0, zr_ext[0:pad], 0.0)
        ext_ref[pad + tm:pad + tm + pad] = jnp.where(i < nt - 1, zr_ext[pad + tm:pad + tm + pad], 0.0)
    else:
        zr_main = _dot(u_main, wr)
        ext_ref[0:pad] = jnp.zeros((pad, RW_COLS_PAD), f32)
        ext_ref[pad + tm:pad + tm + pad] = jnp.zeros((pad, RW_COLS_PAD), f32)
    ext_ref[pad:pad + tm] = zr_main
    def hg_group(n):
        return _dot(u_main, wh_ref[:, n * HG_WIDTH:(n + 1) * HG_WIDTH])

    zh = [hg_group(0), hg_group(1), hg_group(2), hg_group(3)]

    lane = lax.broadcasted_iota(jnp.int32, (tm, LANES), 1)
    if grid_mode:
        col = lax.broadcasted_iota(jnp.int32, (tm, LANES), 0) & (GRID_W - 1)
        not_first_col = col != 0
        not_last_col = col != GRID_W - 1
        bounds = [0, RW_COLS // 4, RW_COLS // 2, 3 * RW_COLS // 4, RW_COLS]
        windows = [(pad - 1, not_first_col), (pad + 1, not_last_col), (pad - GRID_W, None), (pad + GRID_W, None)]
    else:
        bounds = [0, RW_COLS // 2, RW_COLS]
        windows = [(pad - 1, None), (pad + 1, None)]

    zt = []
    for c in range(RW_COLS_PAD // LANES):
        lo, hi = c * LANES, (c + 1) * LANES
        cs = slice(lo, hi)
        z = zr_main[:, cs]
        shifted = jnp.zeros((tm, LANES), f32)
        for q, (off, valid) in enumerate(windows):
            qlo, qhi = bounds[q], bounds[q + 1]
            if max(lo, qlo) >= min(hi, qhi):
                continue
            win = ext_ref[off:off + tm, cs]
            m = valid
            if not (qlo <= lo and hi <= qhi):
                inq = (lane >= qlo - lo) & (lane < qhi - lo)
                m = inq if m is None else (m & inq)
            shifted = win if m is None else jnp.where(m, win, shifted)
        zt.append(z + mu_ref[:, cs] * (shifted - z))

    lane_lo = lane < RW_HEAD
    lora_in = jnp.where(lane_lo, jnp.tanh(zt[LORA_TILE]), zt[LORA_TILE]).astype(bf16)
    pre = _dot(lora_in, wl_ref[...]) + w0a0_ref[...]
    g_all = _dot(_sigmoid(zt[GLORA_TILE]).astype(bf16), g2_ref[...])
    zh.append(hg_group(4))

    for p in range(NPAIR):
        sl = slice(p * LANES, (p + 1) * LANES)
        r = zt[p]
        k = zt[NPAIR + p]
        v = zt[2 * NPAIR + p]
        lw_f = -_sigmoid(pre[:, sl]) * EXP_M05
        lw_b = -_sigmoid(pre[:, RW_WIDTH + p * LANES:RW_WIDTH + (p + 1) * LANES]) * EXP_M05
        a_f = _sigmoid(pre[:, 2 * RW_WIDTH + p * LANES:2 * RW_WIDTH + (p + 1) * LANES])
        a_b = _sigmoid(pre[:, 3 * RW_WIDTH + p * LANES:3 * RW_WIDTH + (p + 1) * LANES])
        kk = k * rwp_ref[0:1, sl]
        kk = kk / jnp.maximum(jnp.sqrt(_seg64_sum(kk * kk, lane_lo)), 1e-12)
        ka = rwp_ref[1:2, sl]
        k_f = k * (1.0 + (a_f - 1.0) * ka)
        k_b = k * (1.0 + (a_b - 1.0) * ka)
        bonus = _seg64_sum(r * rwp_ref[2:3, sl] * (k_f + k_b), lane_lo) * v
        r_o[0, p] = r.astype(bf16)
        v_o[0, p] = v.astype(bf16)
        kk_o[0, p] = kk.astype(bf16)
        lwf_o[0, p] = lw_f
        lwb_o[0, p] = lw_b
        kf_o[0, p] = k_f.astype(bf16)
        kb_o[0, p] = k_b.astype(bf16)
        bf_o[0, p] = (kk * a_f).astype(bf16)
        bb_o[0, p] = (kk * a_b).astype(bf16)
        g_o[0, p] = g_all[:, sl].astype(bf16)
        bonus_o[0, p] = bonus.astype(bf16)

    zq, zff, zfb, zi, zog = zh
    for p in range(NPAIR):
        sl = slice(p * LANES, (p + 1) * LANES)
        hq_o[0, p] = _silu(zq[:, sl]).astype(bf16)
    for p in range(NPAIR):
        sl = slice(p * LANES, (p + 1) * LANES)
        lb_f = lb_ref[0:1, sl]
        hff_o[0, p] = lb_f + (1.0 - lb_f) * _sigmoid(zff[:, sl])
    for p in range(NPAIR):
        sl = slice(p * LANES, (p + 1) * LANES)
        lb_b = lb_ref[1:2, sl]
        hfb_o[0, p] = lb_b + (1.0 - lb_b) * _sigmoid(zfb[:, sl])
        hv_o[0, p] = zi[:, sl].astype(bf16)
    for p in range(NPAIR):
        sl = slice(p * LANES, (p + 1) * LANES)
        hgate_o[0, p] = _silu(zog[:, sl]).astype(bf16)


def _inproj_call(x, shift, scale, wts, *, grid_mode):
    B, T, D = x.shape
    if grid_mode:
        tm, pad = INPROJ_TM, GRID_W
        rows_per_tile = tm // GRID_W
        n_rows = T // GRID_W
        x_specs = [
            pl.BlockSpec((1, tm, D), lambda b, i: (b, i, 0)),
            pl.BlockSpec((1, GRID_W, D), lambda b, i: (b, jnp.maximum(i * rows_per_tile - 1, 0), 0)),
            pl.BlockSpec((1, GRID_W, D), lambda b, i: (b, jnp.minimum((i + 1) * rows_per_tile, n_rows - 1), 0)),
        ]
        x_args = [x, x, x]
    else:
        tm, pad = T, 8
        x_specs = [pl.BlockSpec((1, tm, D), lambda b, i: (b, i, 0))]
        x_args = [x]
    vec_spec = pl.BlockSpec((1, 1, D), lambda b, i: (b, 0, 0))
    w_specs = [_const_spec(w.shape) for w in wts]
    out_spec = pl.BlockSpec((1, NPAIR, tm, LANES), lambda b, i: (b, 0, i, 0))
    out_shape = [jax.ShapeDtypeStruct((B, NPAIR, T, LANES), bf16 if nm in _PREP_BF16 else f32) for nm in _PREP_NAMES]
    return pl.pallas_call(
        functools.partial(_inproj_body, tm=tm, pad=pad, grid_mode=grid_mode),
        grid=(B, T // tm),
        in_specs=x_specs + [vec_spec, vec_spec] + w_specs,
        out_specs=[out_spec] * N_PREP,
        out_shape=out_shape,
        scratch_shapes=[pltpu.VMEM((tm + 2 * pad, RW_COLS_PAD), f32)],
        compiler_params=pltpu.CompilerParams(
            dimension_semantics=("parallel", "parallel"), vmem_limit_bytes=VMEM_LIMIT),
        name="inproj_grid" if grid_mode else "inproj_seq",
    )(*x_args, shift, scale, *wts)


def _cumsum3(x, tri):
    x1, x2, _ = _split3(x)
    return _dot(tri, x1) + _dot(tri, x2)


def _lockstep(gens):
    results = [None] * len(gens)
    active = list(enumerate(gens))
    while active:
        still = []
        for i, g in active:
            try:
                next(g)
                still.append((i, g))
            except StopIteration as stop:
                results[i] = stop.value
        active = still
        if active:
            yield
    return results


def _hgrn_chunk(q, v, f, St, c_ref, *, rev, emit, tri):
    C = CHUNK
    q = q.astype(f32)
    k = 1.0 - f
    b = _cumsum3(jnp.log(f), tri)
    yield
    e = 0 if rev else C - 1
    b_end = b[e:e + 1]
    kd = (k * jnp.exp(b_end - b)).astype(bf16)
    vb = v.astype(bf16)
    dS = _dot_tn(vb, kd)
    if not emit:
        yield
        return None, St * jnp.exp(b_end) + dS

    o_inter = _dot_nt((q * jnp.exp(b)).astype(bf16), St.astype(bf16))
    scs = {}
    for i in range(C // SUB):
        r0 = SUB * i
        if not rev and i > 0:
            ref, lo, hi = b[r0 - 1:r0], 0, r0
        elif rev and i < C // SUB - 1:
            ref, lo, hi = b[r0 + SUB:r0 + SUB + 1], r0 + SUB, C
        else:
            continue
        qi = (q[r0:r0 + SUB] * jnp.exp(b[r0:r0 + SUB] - ref)).astype(bf16)
        kj = (k[lo:hi] * jnp.exp(ref - b[lo:hi])).astype(bf16)
        pads = [jnp.zeros((n, LANES), bf16) for n in (lo, C - hi)]
        kj = jnp.concatenate([a for a in (pads[0], kj, pads[1]) if a.shape[0]], 0)
        scs[i] = _dot_nt(qi, kj)
    b2 = b * LOG2E
    c_ref[...] = b2 - jnp.log2(k)
    yield
    row8 = lax.broadcasted_iota(jnp.int32, (8, LANES), 0)
    lane = lax.broadcasted_iota(jnp.int32, (8, C), 1)
    blocks = []
    for i in range(C // SUB):
        r0 = SUB * i
        halves = [scs[i][0:8], scs[i][8:16]] if i in scs else [jnp.zeros((8, C), f32)] * 2
        for sl_ in range(SUB):
            s = r0 + sl_
            hs, sr = divmod(sl_, 8)
            cs = c_ref[s:s + 1, :]
            for h in (0, 1):
                if (h < hs and not rev) or (h > hs and rev):
                    continue
                h0 = r0 + 8 * h
                d = b2[h0:h0 + 8] - cs
                if h == hs:
                    d = jnp.where((row8 <= sr) if rev else (row8 >= sr), d, -jnp.inf)
                w = jnp.sum(q[h0:h0 + 8] * jnp.exp2(d), -1, keepdims=True)
                halves[h] = jnp.where(lane == s, w, halves[h])
            if sl_ % 8 == 7:
                yield
        blocks += halves
    o_intra = _dot(jnp.concatenate(blocks, 0).astype(bf16), vb)
    yield
    return o_inter + o_intra, St * jnp.exp(b_end) + dS


def _bd(x, lo):
    zero = jnp.zeros_like(x)
    return jnp.concatenate([jnp.where(lo, x, zero), jnp.where(lo, zero, x)], 0)


def _tri_inv(L, lo, eye, blk16, blk32):
    Lb = jnp.where(blk16, L, 0.0)
    T = eye + Lb
    Xb = Lb.astype(bf16)
    X2 = _dot(Xb, _bd(Xb, lo))
    yield
    for level in range(3):
        Xb = X2.astype(bf16)
        Xbd = _bd(Xb, lo)
        TX = _dot(T.astype(bf16), Xbd)
        if level < 2:
            X2 = _dot(Xb, Xbd)
        yield
        T = T + TX
    for off_mask in (blk32 & jnp.logical_not(blk16), jnp.logical_not(blk32)):
        off = jnp.where(off_mask, L, 0.0).astype(bf16)
        Tb = T.astype(bf16)
        A = _dot(Tb, _bd(off, lo))
        yield
        A = _dot(A.astype(bf16), _bd(Tb, lo))
        yield
        T = T + A
    return T


def _rwkv_chunk(r, v, kk, lw, k, bb, Gt, *, rev, emit, tri, masks):
    C = CHUNK
    lo, pmask, bdmask, eye, blk16, blk32 = masks
    r, kk, k, bb = [a.astype(f32) for a in (r, kk, k, bb)]
    g = _cumsum3(lw, tri)
    yield
    e = 0 if rev else C - 1
    g_end = jnp.exp(g[e:e + 1])
    ieg = jnp.exp(-g)
    at = -kk * jnp.exp(g - lw)
    bt, kt = bb * ieg, k * ieg
    if emit:
        AR = jnp.concatenate([at, r * jnp.exp(g)], 0).astype(bf16)
    else:
        AR = at.astype(bf16)
    P2 = _dot_nt(AR, Gt.astype(bf16))
    P1 = _dot_nt(AR, jnp.concatenate([_bd(bt.astype(bf16), lo), _bd(kt.astype(bf16), lo)], 0))
    BKe = (jnp.concatenate([bt, kt], 0) * g_end).astype(bf16)
    yield
    vb = v.astype(bf16)
    vbd = _bd(vb, lo)
    P1 = jnp.where(pmask, P1, 0.0)
    lak = _dot(P1[0:C, 2 * C:4 * C].astype(bf16), vbd)
    if emit:
        Mr = P1[C:2 * C].astype(bf16)
    T = yield from _tri_inv(P1[0:C, 0:2 * C], lo, eye, blk16, blk32)
    Xp = (P2[0:C] + lak).astype(bf16)
    U = _dot(T.astype(bf16), _bd(Xp, lo))
    yield
    Ub = U.astype(bf16)
    dG = _dot_tn(jnp.concatenate([Ub, vb], 0), BKe)
    if emit:
        Y = _dot(Mr, jnp.concatenate([_bd(Ub, lo), vbd], 0))
    yield
    Gt_new = Gt * g_end + jnp.where(bdmask, dG, 0.0)
    if not emit:
        return None, Gt_new
    return P2[C:2 * C] + Y, Gt_new


def _scan_body(*refs, emit):
    fwd = refs[0:9]
    bwd = refs[9:18]
    hs0_ref, gs0_ref = refs[18:20]
    if emit:
        ho_f, ry_f, ho_b, ry_b, hsT_ref, gsT_ref, hs, gs, crow = refs[20:29]
    else:
        hsT_ref, gsT_ref, hs, gs = refs[20:24]
        crow = None
    j = pl.program_id(1)
    C = CHUNK

    @pl.when(j == 0)
    def _():
        hs[...] = hs0_ref[...]
        gs[...] = gs0_ref[...]

    ri = lax.broadcasted_iota(jnp.int32, (C, C), 0)
    ci = lax.broadcasted_iota(jnp.int32, (C, C), 1)
    tri_f = (ci <= ri).astype(bf16)
    tri_b = (ci >= ri).astype(bf16)
    rows = 2 * C if emit else C
    R = lax.broadcasted_iota(jnp.int32, (rows, 4 * C), 0)
    Cc = lax.broadcasted_iota(jnp.int32, (rows, 4 * C), 1)
    t, s = R & (C - 1), Cc & (C - 1)
    diag_ok = (R >= C) & (s == t)
    pmask_f = (s < t) | diag_ok
    pmask_b = (s > t) | diag_ok
    R2 = lax.broadcasted_iota(jnp.int32, (2 * C, 2 * C), 0)
    C2 = lax.broadcasted_iota(jnp.int32, (2 * C, 2 * C), 1)
    bdmask = (R2 < RW_HEAD) == (C2 < RW_HEAD)
    rp = lax.broadcasted_iota(jnp.int32, (C, 2 * C), 0)
    lp = lax.broadcasted_iota(jnp.int32, (C, 2 * C), 1)
    sp = lp & (C - 1)
    lo = lp < RW_HEAD
    eye = (rp == sp).astype(f32)
    blk16 = (rp // SUB) == (sp // SUB)
    blk32 = (rp // (2 * SUB)) == (sp // (2 * SUB))
    masks_f = (lo, pmask_f, bdmask, eye, blk16, blk32)
    masks_b = (lo, pmask_b, bdmask, eye, blk16, blk32)

    rw_chains, hg_chains, where = [], [], []
    for bi in range(hs.shape[1]):
        for p in range(NPAIR):
            for d, (src, rev, tri, masks) in enumerate(
                    ((fwd, False, tri_f, masks_f), (bwd, True, tri_b, masks_b))):
                hq, hv, hf, r, v, kk, lw, k, bb = [ref[bi, p] for ref in src]
                c_ref = crow.at[d, bi, p] if emit else None
                hg_chains.append(_hgrn_chunk(hq, hv, hf, hs[d, bi, p], c_ref, rev=rev, emit=emit, tri=tri))
                rw_chains.append(_rwkv_chunk(r, v, kk, lw, k, bb, gs[d, bi, p], rev=rev, emit=emit, tri=tri,
                                             masks=masks))
                where.append((d, bi, p, rev))
    def delayed(gen, rounds):
        for _ in range(rounds):
            yield
        return (yield from gen)

    per_elem = 2 * NPAIR
    order = rw_chains + hg_chains
    order = [delayed(g, SCAN_STAGGER * ((n % len(where)) // per_elem)) for n, g in enumerate(order)]
    driver = _lockstep(order)
    try:
        while True:
            next(driver)
    except StopIteration as stop:
        results = stop.value
    for n, (d, bi, p, rev) in enumerate(where):
        (y, Gt_new), (o, St_new) = results[n], results[len(where) + n]
        hs[d, bi, p] = St_new
        gs[d, bi, p] = Gt_new
        if emit:
            (ho_b if rev else ho_f)[bi, p] = o.astype(bf16)
            (ry_b if rev else ry_f)[bi, p] = y.astype(bf16)

    @pl.when(j == pl.num_programs(1) - 1)
    def _():
        hsT_ref[...] = hs[...]
        gsT_ref[...] = gs[...]


def _scan_call(prep, hs0, gs0, *, emit):
    B, _, T, _ = prep["hq"].shape
    n = T // CHUNK
    nb = SCAN_BATCH if B % SCAN_BATCH == 0 else 1
    f_spec = pl.BlockSpec((nb, NPAIR, CHUNK, LANES), lambda b, j: (b, 0, j, 0))
    b_spec = pl.BlockSpec((nb, NPAIR, CHUNK, LANES), lambda b, j: (b, 0, n - 1 - j, 0))
    st_spec = pl.BlockSpec((2, nb, NPAIR, LANES, LANES), lambda b, j: (0, b, 0, 0, 0))
    fwd = [prep[nm] for nm in ("hq", "hv", "hff", "r", "v", "kk", "lwf", "kf", "bf")]
    bwd = [prep[nm] for nm in ("hq", "hv", "hfb", "r", "v", "kk", "lwb", "kb", "bb")]
    seq_shape = jax.ShapeDtypeStruct((B, NPAIR, T, LANES), bf16)
    st_shape = jax.ShapeDtypeStruct((2, B, NPAIR, LANES, LANES), f32)
    out_specs = ([f_spec, f_spec, b_spec, b_spec] if emit else []) + [st_spec, st_spec]
    out_shape = ([seq_shape] * 4 if emit else []) + [st_shape, st_shape]
    return pl.pallas_call(
        functools.partial(_scan_body, emit=emit),
        grid=(B // nb, n),
        in_specs=[f_spec] * 9 + [b_spec] * 9 + [st_spec, st_spec],
        out_specs=out_specs,
        out_shape=out_shape,
        scratch_shapes=[pltpu.VMEM((2, nb, NPAIR, LANES, LANES), f32), pltpu.VMEM((2, nb, NPAIR, LANES, LANES), f32)]
        + ([pltpu.VMEM((2, nb, NPAIR, CHUNK, LANES), f32)] if emit else []),
        compiler_params=pltpu.CompilerParams(
            dimension_semantics=("parallel", "arbitrary"), vmem_limit_bytes=VMEM_LIMIT),
        name="scan_latent" if emit else "scan_ctx",
    )(*fwd, *bwd, hs0, gs0)


def _tail_body(hof, hob, hgate, ryf, ryb, g, bonus, x_ref, g1_ref, sh_ref, sc_ref, g2_ref, wout_ref, hgw_ref,
               lnx_ref, ln1_ref, wg_ref, wu_ref, wd_ref, ln2_ref, o_ref, *, alpha):
    tm = x_ref.shape[1] // TAIL_SPLIT
    lane_lo = lax.broadcasted_iota(jnp.int32, (tm, LANES), 1) < RW_HEAD

    def rows_chain(r0):
        rows = slice(r0, r0 + tm)
        y_hg, y_rw = [], []
        for p in range(NPAIR):
            o = hof[0, p, rows].astype(f32) + hob[0, p, rows].astype(f32)
            o = o * lax.rsqrt(jnp.mean(o * o, -1, keepdims=True) + HG_NORM_EPS) * hgw_ref[...]
            y_hg.append((o * hgate[0, p, rows].astype(f32)).astype(bf16))
            yr = ryf[0, p, rows].astype(f32) + ryb[0, p, rows].astype(f32)
            m = _seg64_sum(yr, lane_lo) * (1.0 / RW_HEAD)
            yc = yr - m
            var = _seg64_sum(yc * yc, lane_lo) * (1.0 / RW_HEAD)
            sl = slice(p * LANES, (p + 1) * LANES)
            yn = yc * lax.rsqrt(var + RW_GN_EPS) * lnx_ref[0:1, sl] + lnx_ref[1:2, sl]
            y_rw.append(((yn + bonus[0, p, rows].astype(f32)) * g[0, p, rows].astype(f32)).astype(bf16))
        mix = _dot(jnp.concatenate(y_hg + y_rw, -1), wout_ref[...])
        yield
        x1 = _ln(alpha * x_ref[0, rows] + g1_ref[0] * mix, LN_EPS) * ln1_ref[0:1] + ln1_ref[1:2]
        u = (_ln(x1, ADA_EPS) * (1.0 + sc_ref[0]) + sh_ref[0]).astype(bf16)
        hg, hu = _dot(u, wg_ref[...]), _dot(u, wu_ref[...])
        yield
        ffn = _dot((_silu(hg) * hu).astype(bf16), wd_ref[...])
        yield
        xn = _ln(alpha * x1 + g2_ref[0] * ffn, LN_EPS)
        o_ref[0, rows] = xn * ln2_ref[0:1] + ln2_ref[1:2]

    def delayed(gen, rounds):
        for _ in range(rounds):
            yield
        return (yield from gen)

    for _ in _lockstep([delayed(rows_chain(n * tm), n) for n in range(TAIL_SPLIT)]):
        pass


def _tail_call(scan_outs, prep, x, g1, sh2, sc2, g2, wout, hgw, lnx, ln1, wg, wu, wd, ln2, *, alpha):
    B, T, D = x.shape
    tm = TAIL_TM
    seq_spec = pl.BlockSpec((1, NPAIR, tm, LANES), lambda b, i: (b, 0, i, 0))
    x_spec = pl.BlockSpec((1, tm, D), lambda b, i: (b, i, 0))
    vec_spec = pl.BlockSpec((1, 1, D), lambda b, i: (b, 0, 0))
    ho_f, ry_f, ho_b, ry_b = scan_outs
    consts = (wout, hgw, lnx, ln1, wg, wu, wd, ln2)
    return pl.pallas_call(
        functools.partial(_tail_body, alpha=alpha),
        grid=(B, T // tm),
        in_specs=[seq_spec] * 7 + [x_spec] + [vec_spec] * 4 + [_const_spec(a.shape) for a in consts],
        out_specs=x_spec,
        out_shape=jax.ShapeDtypeStruct((B, T, D), f32),
        compiler_params=pltpu.CompilerParams(
            dimension_semantics=("parallel", "parallel"), vmem_limit_bytes=VMEM_LIMIT),
        name="mix_ffn",
    )(ho_f, ho_b, prep["hgate"], ry_f, ry_b, prep["g"], prep["bonus"], x, g1, sh2, sc2, g2, *consts)


def kernel(x, c, ctx, c_ctx, w_ada, b_ada, w_in, hgrn_lb_logits, hgrn_norm_w, rwkv_mu, rwkv_w0, rwkv_w2, rwkv_a0, rwkv_a2, rwkv_g2, rwkv_k_k, rwkv_k_a, rwkv_r_k, rwkv_lnx_w, rwkv_lnx_b, w_out, ln1_g, ln1_b, w_ffn_gate, w_ffn_up, w_ffn_down, ln2_g, ln2_b):
    B, T, D = x.shape
    depth = w_ada.shape[0]
    assert depth == 1 and T % INPROJ_TM == 0 and T % TAIL_TM == 0 and ctx.shape[1] % CHUNK == 0
    alpha = (2.0 * depth) ** 0.25

    lb = jnp.cumsum(jax.nn.softmax(hgrn_lb_logits.astype(f32), axis=0), axis=0)[0]
    w_h = w_in[0, :, :HG_COLS].astype(bf16)
    w_r = jnp.pad(w_in[0, :, HG_COLS:], ((0, 0), (0, RW_COLS_PAD - RW_COLS))).astype(bf16)
    mu = jnp.pad(rwkv_mu[0], (0, RW_COLS_PAD - RW_COLS))[None]
    w_lora = jnp.zeros((LANES, 4 * RW_WIDTH), f32)
    for blk, w in enumerate((rwkv_w2[0, 0], rwkv_w2[0, 1], rwkv_a2[0, 0], rwkv_a2[0, 1])):
        w_lora = w_lora.at[32 * blk:32 * (blk + 1), RW_WIDTH * blk:RW_WIDTH * (blk + 1)].set(w)
    w_lora = w_lora.astype(bf16)
    w0a0 = jnp.concatenate([rwkv_w0[0, 0], rwkv_w0[0, 1], rwkv_a0[0, 0], rwkv_a0[0, 1]])[None]
    g2p = jnp.pad(rwkv_g2[0], ((0, LANES - rwkv_g2.shape[1]), (0, 0))).astype(bf16)
    rwp = jnp.zeros((8, RW_WIDTH), f32).at[0].set(rwkv_k_k[0]).at[1].set(rwkv_k_a[0]).at[2].set(
        rwkv_r_k[0].reshape(-1))
    wts = (w_h, w_r, mu, lb, w_lora, w0a0, g2p, rwp)

    cs = jnp.zeros((16, D), f32).at[:B].set(c).at[B].set(c_ctx)
    mod = _mod_call(cs, w_ada[0], b_ada[0][None])
    sh1, sc1, g1, sh2, sc2, g2 = [m[:B, None, :] for m in jnp.split(mod, 6, axis=-1)]
    ch1, cs1 = [jnp.broadcast_to(m[B:B + 1, None, :], (B, 1, D)) for m in jnp.split(mod, 6, axis=-1)[:2]]

    prep_ctx = dict(zip(_PREP_NAMES, _inproj_call(ctx, ch1, cs1, wts, grid_mode=False)))
    zeros_state = jnp.zeros((2, B, NPAIR, LANES, LANES), f32)
    hs_ctx, gs_ctx = _scan_call(prep_ctx, zeros_state, zeros_state, emit=False)

    prep = dict(zip(_PREP_NAMES, _inproj_call(x, sh1, sc1, wts, grid_mode=True)))
    ho_f, ry_f, ho_b, ry_b, _, _ = _scan_call(prep, hs_ctx, gs_ctx, emit=True)

    lnx = jnp.stack([rwkv_lnx_w[0], rwkv_lnx_b[0]])
    ln1 = jnp.stack([ln1_g[0], ln1_b[0]])
    ln2 = jnp.stack([ln2_g[0], ln2_b[0]])
    return _tail_call((ho_f, ry_f, ho_b, ry_b), prep, x, g1, sh2, sc2, g2, w_out[0].astype(bf16),
                      hgrn_norm_w[0][None], lnx, ln1, w_ffn_gate[0].astype(bf16), w_ffn_up[0].astype(bf16),
                      w_ffn_down[0].astype(bf16), ln2, alpha=alpha)
```
